```python
import jax, jax.numpy as jnp
from jax import lax
import numpy as np

D_MODEL = 1024
BATCH = 1
SEQ = 16384
DEPTH = 1
DEC_BATCH = 128
DEC_SEQ = 8
PAST_LEN = 16384
PAGE_SIZE = 128

HEAD_DIM = 64
DSA_WIDTH = D_MODEL // 2
DSA_HEADS = DSA_WIDTH // HEAD_DIM
DSA_KV_HEADS = DSA_HEADS // 2
DSA_GROUP = DSA_HEADS // DSA_KV_HEADS
IDX_HEADS = 4
IDX_DIM = 64
TOPK_MAX = 256
MLA_V = 64
MLA_WIDTH = D_MODEL - DSA_WIDTH
MLA_HEADS = MLA_WIDTH // MLA_V
MLA_NOPE = 64
MLA_ROPE = 32
MLA_Q_LORA = 384
MLA_KV_LORA = 256
MLA_SCALE = (MLA_NOPE + MLA_ROPE) ** -0.5
PEER_HEADS = 8
PEER_NKEYS = 128
N_EXPERTS = PEER_NKEYS * PEER_NKEYS
PEER_KEY_DIM = 256
PEER_HALF = PEER_KEY_DIM // 2
PEER_TOPK = 16
ROPE_THETA = 10000.0
EPS = 1e-6
Q_BLOCK = 128
PEER_BLOCK = 128
IN_SPLITS = (DSA_WIDTH, DSA_KV_HEADS * HEAD_DIM, DSA_KV_HEADS * HEAD_DIM,
             IDX_HEADS * IDX_DIM, IDX_HEADS, IDX_DIM,
             MLA_Q_LORA, MLA_KV_LORA, MLA_ROPE)
IN_COLS = sum(IN_SPLITS)

kernel_name = 'dsa_mla_peer_hybrid_step'


def rmsnorm(x, g):
    xf = x.astype(jnp.float32)
    y = xf * lax.rsqrt(jnp.mean(xf * xf, axis=-1, keepdims=True) + EPS)
    return (y * g.astype(jnp.float32)).astype(x.dtype)


def rope(x, pos):
    half = x.shape[-1] // 2
    inv = ROPE_THETA ** (-jnp.arange(half, dtype=jnp.float32) / half)
    ang = pos.astype(jnp.float32)[:, None] * inv[None, :]
    ang = ang.reshape(ang.shape[:1] + (1,) * (x.ndim - 3) + (half,))
    cos, sin = jnp.cos(ang), jnp.sin(ang)
    xf = x.astype(jnp.float32)
    x1, x2 = xf[..., :half], xf[..., half:]
    return jnp.concatenate([x1 * cos - x2 * sin, x2 * cos + x1 * sin], axis=-1).astype(x.dtype)


def adaln(c, w_mod, b_mod):
    m = (jax.nn.silu(c) @ w_mod + b_mod)[:, None, :]
    return jnp.split(m, 6, axis=-1)


def modulate(x, g, shift, scale):
    return rmsnorm(x, g) * (1 + scale) + shift


def mixer_projections(h, pos, w_in, g_cq, g_ckv, w_uq):
    b, t, _ = h.shape
    cuts = [int(v) for v in np.cumsum(IN_SPLITS)[:-1]]
    q, k, v, iq, iw, ik, cq, ckv, kr = jnp.split(h @ w_in, cuts, axis=-1)
    q = rope(q.reshape(b, t, DSA_HEADS, HEAD_DIM), pos)
    k = rope(k.reshape(b, t, DSA_KV_HEADS, HEAD_DIM), pos)
    v = v.reshape(b, t, DSA_KV_HEADS, HEAD_DIM)
    iq = rope(iq.reshape(b, t, IDX_HEADS, IDX_DIM), pos)
    ik = rope(ik, pos)
    iw = iw * IDX_HEADS ** -0.5
    ckv = rmsnorm(ckv, g_ckv)
    kr = rope(kr, pos)
    qm = jnp.einsum('btc,chd->bthd', rmsnorm(cq, g_cq), w_uq)
    qn = qm[..., :MLA_NOPE]
    qr = rope(qm[..., MLA_NOPE:], pos)
    return q, k, v, iq, iw, ik, qn, qr, ckv, kr


def indexer_select(iq, iw, ik, q_pos, n_sel):
    rel = jax.nn.relu(jnp.einsum('thd,sd->ths', iq, ik))
    score = jnp.einsum('th,ths->ts', iw, rel).astype(jnp.float32) * IDX_DIM ** -0.5
    visible = jnp.arange(ik.shape[0])[None, :] <= q_pos[:, None]
    _, sel = lax.top_k(jnp.where(visible, score, -jnp.inf), n_sel)
    valid = jnp.take_along_axis(visible, sel, axis=1)
    return sel, valid


def sparse_attend(q, k_sel, v_sel, valid):
    tq = q.shape[0]
    qg = q.reshape(tq, DSA_KV_HEADS, DSA_GROUP, HEAD_DIM)
    s = jnp.einsum('tgrd,tkgd->tgrk', qg, k_sel).astype(jnp.float32) * HEAD_DIM ** -0.5
    p = jax.nn.softmax(jnp.where(valid[:, None, None, :], s, -jnp.inf), axis=-1).astype(v_sel.dtype)
    return jnp.einsum('tgrk,tkgd->tgrd', p, v_sel).reshape(tq, DSA_WIDTH)


def to_blocks(a, nb):
    return a.reshape((nb, Q_BLOCK) + a.shape[1:])


def dsa_prompt_seq(q, k, v, iq, iw, ik):
    s_len = q.shape[0]
    nb = s_len // Q_BLOCK
    n_sel = min(TOPK_MAX, s_len // 4)

    def block(args):
        qb, iqb, iwb, pb = args
        sel, valid = indexer_select(iqb, iwb, ik, pb, n_sel)
        return sparse_attend(qb, k[sel], v[sel], valid)

    pos = jnp.arange(s_len)
    out = lax.map(block, (to_blocks(q, nb), to_blocks(iq, nb), to_blocks(iw, nb), to_blocks(pos, nb)))
    return out.reshape(s_len, DSA_WIDTH)


def mla_prompt_seq(qn, qr, ckv, kr, w_uk, w_uv):
    s_len = qn.shape[0]
    nb = s_len // Q_BLOCK
    k_nope = jnp.einsum('sc,chd->shd', ckv, w_uk)
    val = jnp.einsum('sc,chd->shd', ckv, w_uv)
    kpos = jnp.arange(s_len)

    def block(args):
        qnb, qrb, pb = args
        s = (jnp.einsum('thd,shd->hts', qnb, k_nope)
             + jnp.einsum('thd,sd->hts', qrb, kr)).astype(jnp.float32) * MLA_SCALE
        s = jnp.where(kpos[None, None, :] <= pb[None, :, None], s, -jnp.inf)
        p = jax.nn.softmax(s, axis=-1).astype(val.dtype)
        return jnp.einsum('hts,shd->thd', p, val).reshape(Q_BLOCK, MLA_WIDTH)

    out = lax.map(block, (to_blocks(qn, nb), to_blocks(qr, nb), to_blocks(kpos, nb)))
    return out.reshape(s_len, MLA_WIDTH)


def sample_seq(args, l, cache_dsa_k, cache_dsa_v, cache_idx_k, cache_mla_ckv, cache_mla_krope, w_uk, w_uv):
    pt, q, k_new, v_new, iq, iw, ik_new, qn, qr, ckv_new, kr_new = args
    t = q.shape[0]
    l_keys = PAST_LEN + t
    q_pos = PAST_LEN + jnp.arange(t)

    def past(cache):
        rows = cache[l, pt]
        return rows.reshape((PAST_LEN,) + rows.shape[2:])

    ik_all = jnp.concatenate([past(cache_idx_k), ik_new], axis=0)
    sel, valid = indexer_select(iq, iw, ik_all, q_pos, min(TOPK_MAX, l_keys // 4))
    in_past = (sel < PAST_LEN)[..., None, None]
    sp = jnp.minimum(sel, PAST_LEN - 1)
    phys, off = pt[sp // PAGE_SIZE], sp % PAGE_SIZE
    sn = jnp.clip(sel - PAST_LEN, 0, t - 1)
    k_sel = jnp.where(in_past, cache_dsa_k[l, phys, off], k_new[sn])
    v_sel = jnp.where(in_past, cache_dsa_v[l, phys, off], v_new[sn])
    o_dsa = sparse_attend(q, k_sel, v_sel, valid)

    ckv_all = jnp.concatenate([past(cache_mla_ckv), ckv_new], axis=0)
    kr_all = jnp.concatenate([past(cache_mla_krope), kr_new], axis=0)
    q_lat = jnp.einsum('thd,chd->thc', qn, w_uk)
    s = (jnp.einsum('thc,sc->hts', q_lat, ckv_all)
         + jnp.einsum('thd,sd->hts', qr, kr_all)).astype(jnp.float32) * MLA_SCALE
    s = jnp.where(jnp.arange(l_keys)[None, None, :] <= q_pos[None, :, None], s, -jnp.inf)
    p = jax.nn.softmax(s, axis=-1).astype(ckv_all.dtype)
    o_lat = jnp.einsum('hts,sc->thc', p, ckv_all)
    o_mla = jnp.einsum('thc,chd->thd', o_lat, w_uv).reshape(t, MLA_WIDTH)
    return o_dsa, o_mla


def merge_groups(o_dsa, o_mla, g_out_dsa, g_out_mla, w_out):
    return jnp.concatenate([rmsnorm(o_dsa, g_out_dsa), rmsnorm(o_mla, g_out_mla)], axis=-1) @ w_out


def peer(h, w_pq, peer_keys, peer_u, peer_v):
    n, d = h.shape
    hp = jnp.pad(h, ((0, (-n) % PEER_BLOCK), (0, 0)))

    def block(hb):
        qry = (hb @ w_pq).reshape(PEER_BLOCK, PEER_HEADS, 2, PEER_HALF)
        sub = jnp.einsum('thpd,hpnd->thpn', qry, peer_keys).astype(jnp.float32)
        s_top, i_top = lax.top_k(sub, PEER_TOPK)
        cand_s = (s_top[:, :, 0, :, None] + s_top[:, :, 1, None, :]).reshape(
            PEER_BLOCK, PEER_HEADS, PEER_TOPK * PEER_TOPK)
        cand_e = (i_top[:, :, 0, :, None] * PEER_NKEYS + i_top[:, :, 1, None, :]).reshape(
            PEER_BLOCK, PEER_HEADS, PEER_TOPK * PEER_TOPK)
        g_s, g_i = lax.top_k(cand_s, PEER_TOPK)
        eid = jnp.take_along_axis(cand_e, g_i, axis=-1)
        g = jax.nn.softmax(g_s, axis=-1).astype(hb.dtype)
        act = jax.nn.gelu(jnp.einsum('td,thkd->thk', hb, peer_u[eid]), approximate=False)
        return jnp.einsum('thk,thkd->td', g * act, peer_v[eid])

    out = lax.map(block, hp.reshape(-1, PEER_BLOCK, d))
    return out.reshape(-1, d)[:n]


def channel_mix(x, shift, scale, g_ffn, w_pq, peer_keys, peer_u, peer_v):
    b, t, d = x.shape
    h = modulate(x, g_ffn, shift, scale)
    return peer(h.reshape(b * t, d), w_pq, peer_keys, peer_u, peer_v).reshape(b, t, d)


def setup_inputs(seed: int = 0) -> dict:
    key = jax.random.key(seed)
    ks = jax.random.split(key, 32)
    f32 = jnp.float32
    n_pages = PAST_LEN // PAGE_SIZE
    n_used = DEC_BATCH * n_pages
    n_pool = n_used + max(1, n_used // 4)
    page_table = jax.random.permutation(ks[0], n_pool)[:n_used].reshape(DEC_BATCH, n_pages).astype(jnp.int32)

    def nrm(k, shape, s):
        return s * jax.random.normal(k, shape, f32)

    def gain(k, shape):
        return 1.0 + 0.05 * jax.random.normal(k, shape, f32)

    return {
        'x_prompt': jax.random.normal(ks[1], (BATCH, SEQ, D_MODEL), f32),
        'x_sample': jax.random.normal(ks[2], (DEC_BATCH, DEC_SEQ, D_MODEL), f32),
        'cache_dsa_k': jax.random.normal(ks[3], (DEPTH, n_pool, PAGE_SIZE, DSA_KV_HEADS, HEAD_DIM), f32),
        'cache_dsa_v': jax.random.normal(ks[4], (DEPTH, n_pool, PAGE_SIZE, DSA_KV_HEADS, HEAD_DIM), f32),
        'cache_idx_k': jax.random.normal(ks[5], (DEPTH, n_pool, PAGE_SIZE, IDX_DIM), f32),
        'cache_mla_ckv': jax.random.normal(ks[6], (DEPTH, n_pool, PAGE_SIZE, MLA_KV_LORA), f32),
        'cache_mla_krope': jax.random.normal(ks[7], (DEPTH, n_pool, PAGE_SIZE, MLA_ROPE), f32),
        'page_table': page_table,
        'c_prompt': jax.random.normal(ks[8], (BATCH, D_MODEL), f32),
        'c_sample': jax.random.normal(ks[9], (DEC_BATCH, D_MODEL), f32),
        'w_mod': nrm(ks[10], (DEPTH, D_MODEL, 6 * D_MODEL), 0.5 * D_MODEL ** -0.5),
        'b_mod': nrm(ks[11], (DEPTH, 6 * D_MODEL), 0.02),
        'g_mix': gain(ks[12], (DEPTH, D_MODEL)),
        'g_ffn': gain(ks[13], (DEPTH, D_MODEL)),
        'w_in': nrm(ks[14], (DEPTH, D_MODEL, IN_COLS), D_MODEL ** -0.5),
        'g_cq': gain(ks[15], (DEPTH, MLA_Q_LORA)),
        'g_ckv': gain(ks[16], (DEPTH, MLA_KV_LORA)),
        'w_uq': nrm(ks[17], (DEPTH, MLA_Q_LORA, MLA_HEADS, MLA_NOPE + MLA_ROPE), MLA_Q_LORA ** -0.5),
        'w_uk': nrm(ks[18], (DEPTH, MLA_KV_LORA, MLA_HEADS, MLA_NOPE), MLA_KV_LORA ** -0.5),
        'w_uv': nrm(ks[19], (DEPTH, MLA_KV_LORA, MLA_HEADS, MLA_V), MLA_KV_LORA ** -0.5),
        'g_out_dsa': gain(ks[20], (DEPTH, DSA_WIDTH)),
        'g_out_mla': gain(ks[21], (DEPTH, MLA_WIDTH)),
        'w_out': nrm(ks[22], (DEPTH, D_MODEL, D_MODEL), D_MODEL ** -0.5),
        'w_pq': nrm(ks[23], (DEPTH, D_MODEL, PEER_HEADS * PEER_KEY_DIM), D_MODEL ** -0.5),
        'peer_keys': nrm(ks[24], (DEPTH, PEER_HEADS, 2, PEER_NKEYS, PEER_HALF), PEER_HALF ** -0.5),
        'peer_u': nrm(ks[25], (DEPTH, N_EXPERTS, D_MODEL), D_MODEL ** -0.5),
        'peer_v': nrm(ks[26], (DEPTH, N_EXPERTS, D_MODEL), PEER_HEADS ** -0.5),
        'g_final': gain(ks[27], (D_MODEL,)),
    }


def reference(x_prompt, x_sample, cache_dsa_k, cache_dsa_v, cache_idx_k, cache_mla_ckv, cache_mla_krope,
              page_table, c_prompt, c_sample, w_mod, b_mod, g_mix, g_ffn, w_in, g_cq, g_ckv, w_uq, w_uk,
              w_uv, g_out_dsa, g_out_mla, w_out, w_pq, peer_keys, peer_u, peer_v, g_final):
    pos_p = jnp.arange(x_prompt.shape[1])
    pos_s = PAST_LEN + jnp.arange(x_sample.shape[1])
    xp, xs = x_prompt, x_sample
    new_p = [[], [], [], [], []]
    new_s = [[], [], [], [], []]
    for l in range(DEPTH):
        w_uk_l, w_uv_l = w_uk[l], w_uv[l]
        mp = adaln(c_prompt, w_mod[l], b_mod[l])
        ms = adaln(c_sample, w_mod[l], b_mod[l])

        q, k, v, iq, iw, ik, qn, qr, ckv, kr = mixer_projections(
            modulate(xp, g_mix[l], mp[0], mp[1]), pos_p, w_in[l], g_cq[l], g_ckv[l], w_uq[l])
        o_dsa = lax.map(lambda a: dsa_prompt_seq(*a), (q, k, v, iq, iw, ik))
        o_mla = lax.map(lambda a: mla_prompt_seq(*a, w_uk_l, w_uv_l), (qn, qr, ckv, kr))
        xp = xp + mp[2] * merge_groups(o_dsa, o_mla, g_out_dsa[l], g_out_mla[l], w_out[l])
        xp = xp + mp[5] * channel_mix(xp, mp[3], mp[4], g_ffn[l], w_pq[l], peer_keys[l], peer_u[l], peer_v[l])
        for lst, val in zip(new_p, (k, v, ik, ckv, kr)):
            lst.append(val)

        q, k, v, iq, iw, ik, qn, qr, ckv, kr = mixer_projections(
            modulate(xs, g_mix[l], ms[0], ms[1]), pos_s, w_in[l], g_cq[l], g_ckv[l], w_uq[l])
        o_dsa, o_mla = lax.map(
            lambda a: sample_seq(a, l, cache_dsa_k, cache_dsa_v, cache_idx_k, cache_mla_ckv,
                                 cache_mla_krope, w_uk_l, w_uv_l),
            (page_table, q, k, v, iq, iw, ik, qn, qr, ckv, kr))
        xs = xs + ms[2] * merge_groups(o_dsa, o_mla, g_out_dsa[l], g_out_mla[l], w_out[l])
        xs = xs + ms[5] * channel_mix(xs, ms[3], ms[4], g_ffn[l], w_pq[l], peer_keys[l], peer_u[l], peer_v[l])
        for lst, val in zip(new_s, (k, v, ik, ckv, kr)):
            lst.append(val)

    y_prompt = rmsnorm(xp, g_final)
    y_sample = rmsnorm(xs, g_final)
    sp = [jnp.stack(a, axis=0) for a in new_p]
    ss = [jnp.stack(a, axis=0) for a in new_s]
    return (y_prompt, y_sample, sp[0], sp[1], sp[2], sp[3], sp[4], ss[0], ss[1], ss[2], ss[3], ss[4])
```

```python
import functools

import numpy as np
import jax
import jax.numpy as jnp
from jax import lax
from jax.experimental import pallas as pl
from jax.experimental.pallas import tpu as pltpu

F32 = jnp.float32
BF16 = jnp.bfloat16
I32 = jnp.int32

D_MODEL = 1024
HEAD_DIM = 64
DSA_WIDTH = 512
DSA_HEADS = 8
DSA_KV_HEADS = 4
DSA_GROUP = 2
KV_W = DSA_KV_HEADS * HEAD_DIM
IDX_HEADS = 4
IDX_DIM = 64
TOPK_MAX = 256
MLA_V = 64
MLA_WIDTH = 512
MLA_HEADS = 8
MLA_NOPE = 64
MLA_ROPE = 32
MLA_Q_LORA = 384
MLA_KV_LORA = 256
MLA_SCALE = (MLA_NOPE + MLA_ROPE) ** -0.5
PEER_HEADS = 8
PEER_NKEYS = 128
N_EXPERTS = PEER_NKEYS * PEER_NKEYS
PEER_HALF = 128
PEER_TOPK = 16
PAGE_SIZE = 128
ROPE_THETA = 10000.0
EPS = 1e-6
IN_SPLITS = (DSA_WIDTH, KV_W, KV_W, IDX_HEADS * IDX_DIM, IDX_HEADS, IDX_DIM,
             MLA_Q_LORA, MLA_KV_LORA, MLA_ROPE)

LANES = 128
SUBLANES = 8
VMEM_LIMIT = 56 * 1024 * 1024

C_Q, C_K, C_IQ, C_D, C_V, C_CKV, C_CQ, C_IW, C_END = 0, 512, 768, 1024, 1152, 1408, 1664, 2048, 2176
ROT_W = C_V
QCAT_W = 384
INT_MIN = -2147483648
INT_MAX = 2147483647
NEG_BIG = -1e30


def _cparams(sem):
    return pltpu.CompilerParams(dimension_semantics=sem, vmem_limit_bytes=VMEM_LIMIT)


def _full_spec(shape):
    nd = len(shape)
    return pl.BlockSpec(shape, lambda *_: (0,) * nd)


def _adaln_kernel(c_ref, w_ref, b_ref, o_ref):
    c = c_ref[...]
    a = c / (1.0 + jnp.exp(-c))
    o_ref[...] = jnp.dot(a, w_ref[...], preferred_element_type=F32,
                         precision=lax.Precision.HIGHEST) + b_ref[...]


def _adaln(c, w_mod, b_mod):
    n = c.shape[0]
    n_pad = -(-n // SUBLANES) * SUBLANES
    cp = jnp.pad(c, ((0, n_pad - n), (0, 0)))
    tn = 1536
    out = pl.pallas_call(
        _adaln_kernel,
        grid=(w_mod.shape[1] // tn,),
        in_specs=[pl.BlockSpec((n_pad, D_MODEL), lambda j: (0, 0)),
                  pl.BlockSpec((D_MODEL, tn), lambda j: (0, j)),
                  pl.BlockSpec((1, tn), lambda j: (0, j))],
        out_specs=pl.BlockSpec((n_pad, tn), lambda j: (0, j)),
        out_shape=jax.ShapeDtypeStruct((n_pad, w_mod.shape[1]), F32),
        compiler_params=_cparams(("arbitrary",)),
        name="adaln",
    )(cp, w_mod, b_mod.reshape(1, -1))
    return out[:n]


def _swap_halves(w, hd):
    d, n = w.shape
    w4 = w.reshape(d, n // hd, 2, hd // 2)
    return w4[:, :, ::-1, :].reshape(d, n)


def _pack_in_weights(w_in):
    cuts = [int(v) for v in np.cumsum(IN_SPLITS)[:-1]]
    wq, wk, wv, wiq, wiw, wik, wcq, wckv, wkr = jnp.split(w_in, cuts, axis=1)
    wq = wq.reshape(D_MODEL, DSA_KV_HEADS, DSA_GROUP, HEAD_DIM).transpose(0, 2, 1, 3).reshape(D_MODEL, DSA_WIDTH)
    pad32 = jnp.zeros((D_MODEL, 32), F32)
    w_d = jnp.concatenate([wik, wkr, pad32], axis=1)
    w_d_rot = jnp.concatenate([_swap_halves(wik, IDX_DIM), _swap_halves(wkr, MLA_ROPE), pad32], axis=1)
    wiw_p = jnp.pad(wiw, ((0, 0), (0, LANES - IDX_HEADS)))
    w_main = jnp.concatenate([wq, wk, wiq, w_d, wv, wckv, wcq, wiw_p], axis=1)
    w_rot = jnp.concatenate([_swap_halves(wq, HEAD_DIM), _swap_halves(wk, HEAD_DIM),
                             _swap_halves(wiq, IDX_DIM), w_d_rot], axis=1)
    return w_main.astype(BF16), w_rot.astype(BF16)


def _pack_mla_weights(w_uq, w_uk, w_uv):
    w_n = w_uq[:, :, :MLA_NOPE].reshape(MLA_Q_LORA, MLA_HEADS * MLA_NOPE)
    w_r = w_uq[:, :, MLA_NOPE:]
    w_r_rot = jnp.concatenate([w_r[..., MLA_ROPE // 2:], w_r[..., :MLA_ROPE // 2]], axis=-1)

    def place(w):
        z = jnp.zeros((MLA_Q_LORA, MLA_HEADS, LANES), F32)
        return z.at[:, :, IDX_DIM:IDX_DIM + MLA_ROPE].set(w).reshape(MLA_Q_LORA, MLA_HEADS * LANES)

    eye = jnp.eye(MLA_HEADS, dtype=F32)
    uk = jnp.einsum('chd,hk->hdkc', w_uk, eye).reshape(MLA_HEADS * MLA_NOPE, MLA_HEADS * MLA_KV_LORA)
    uv = jnp.einsum('chd,hk->hckd', w_uv, eye).reshape(MLA_HEADS * MLA_KV_LORA, MLA_HEADS * MLA_V)
    return (w_n.astype(BF16), place(w_r).astype(BF16), place(w_r_rot).astype(BF16),
            uk.astype(BF16), uv.astype(BF16))


def _rope_tables(pos):
    pos = pos.astype(F32)[:, None]

    def cs(dim):
        half = dim // 2
        inv = ROPE_THETA ** (-jnp.arange(half, dtype=F32) / half)
        ang = pos * inv[None, :]
        c, s = jnp.cos(ang), jnp.sin(ang)
        return jnp.concatenate([c, c], axis=1), jnp.concatenate([-s, s], axis=1)

    c64, s64 = cs(HEAD_DIM)
    c32, s32 = cs(MLA_ROPE)
    z = jnp.zeros((pos.shape[0], 32), F32)
    return (jnp.concatenate([c64, c64, c64, c32, z], axis=1),
            jnp.concatenate([s64, s64, s64, s32, z], axis=1))


def _rms(x):
    return x * lax.rsqrt(jnp.mean(x * x, axis=-1, keepdims=True) + EPS)


def _proj_kernel(x_ref, shift_ref, scale_ref, gmix_ref, wm_ref, wr_ref, cos_ref, sin_ref,
                 gcq_ref, gckv_ref, wqn_ref, wqr_ref, wqrr_ref, wuk_ref,
                 k32_ref, v32_ref, ik32_ref, ckv32_ref, kr32_ref,
                 q16_ref, iq16_ref, iw_ref, k16_ref, v16_ref, ik16_ref, kcat_ref, qcat_ref):
    x = x_ref[...]
    h = _rms(x) * gmix_ref[...] * (1.0 + scale_ref[...]) + shift_ref[...]
    hb = h.astype(BF16)
    main = jnp.dot(hb, wm_ref[...], preferred_element_type=F32)
    rot = jnp.dot(hb, wr_ref[...], preferred_element_type=F32)
    cos = cos_ref[...]
    sin = sin_ref[...]
    cos_h, sin_h = cos[:, :LANES], sin[:, :LANES]
    cos_d, sin_d = cos[:, LANES:], sin[:, LANES:]

    def rope(lo, hi):
        reps = (hi - lo) // LANES
        c = jnp.concatenate([cos_h] * reps, axis=1)
        s = jnp.concatenate([sin_h] * reps, axis=1)
        return main[:, lo:hi] * c + rot[:, lo:hi] * s

    q = rope(C_Q, C_K)
    k = rope(C_K, C_IQ)
    iq = rope(C_IQ, C_D)
    sec_d = main[:, C_D:C_V] * cos_d + rot[:, C_D:C_V] * sin_d
    v = main[:, C_V:C_CKV]
    ckv = _rms(main[:, C_CKV:C_CQ]) * gckv_ref[...]
    cqn = (_rms(main[:, C_CQ:C_IW]) * gcq_ref[...]).astype(BF16)

    k32_ref[...] = k
    v32_ref[...] = v
    ik32_ref[...] = sec_d[:, :IDX_DIM]
    ckv32_ref[...] = ckv
    kr32_ref[...] = sec_d[:, IDX_DIM:IDX_DIM + MLA_ROPE]
    q16_ref[...] = (q * HEAD_DIM ** -0.5).astype(BF16)
    iq16_ref[...] = iq.astype(BF16)
    iw_ref[...] = main[:, C_IW:C_END] * (IDX_HEADS ** -0.5 * IDX_DIM ** -0.5)
    k16_ref[...] = k.astype(BF16)
    v16_ref[...] = v.astype(BF16)
    ik16_ref[...] = sec_d[:, :IDX_DIM].astype(BF16)
    lane = lax.broadcasted_iota(I32, sec_d.shape, 1)
    kr_blk = jnp.where((lane >= IDX_DIM) & (lane < IDX_DIM + MLA_ROPE), sec_d, 0.0)
    kcat_ref[...] = jnp.concatenate([ckv, kr_blk], axis=1).astype(BF16)

    qn = jnp.dot(cqn, wqn_ref[...], preferred_element_type=F32).astype(BF16)
    qlat = jnp.dot(qn, wuk_ref[...], preferred_element_type=F32) * MLA_SCALE
    qr_m = jnp.dot(cqn, wqr_ref[...], preferred_element_type=F32)
    qr_r = jnp.dot(cqn, wqrr_ref[...], preferred_element_type=F32)
    cd = jnp.concatenate([cos_d] * MLA_HEADS, axis=1)
    sd = jnp.concatenate([sin_d] * MLA_HEADS, axis=1)
    qr = (qr_m * cd + qr_r * sd) * MLA_SCALE
    for hh in range(MLA_HEADS):
        qcat_ref[:, hh * QCAT_W:hh * QCAT_W + MLA_KV_LORA] = (
            qlat[:, hh * MLA_KV_LORA:(hh + 1) * MLA_KV_LORA].astype(BF16))
        qcat_ref[:, hh * QCAT_W + MLA_KV_LORA:(hh + 1) * QCAT_W] = (
            qr[:, hh * LANES:(hh + 1) * LANES].astype(BF16))


def _project(x, shift, scale, g_mix, packed, cos_t, sin_t, g_cq, g_ckv, tm=256):
    t = x.shape[0]
    w_main, w_rot, wqn, wqr, wqrr, wuk = packed
    per_tok = shift.shape[0] != 1
    mod_spec = pl.BlockSpec((tm, D_MODEL), lambda i: (i, 0)) if per_tok else _full_spec((1, D_MODEL))

    def row(w):
        return pl.BlockSpec((tm, w), lambda i: (i, 0))

    widths = [(KV_W, F32), (KV_W, F32), (IDX_DIM, F32), (MLA_KV_LORA, F32), (MLA_ROPE, F32),
              (DSA_WIDTH, BF16), (IDX_HEADS * IDX_DIM, BF16), (LANES, F32), (KV_W, BF16), (KV_W, BF16),
              (IDX_DIM, BF16), (QCAT_W, BF16), (MLA_HEADS * QCAT_W, BF16)]
    return pl.pallas_call(
        _proj_kernel,
        grid=(t // tm,),
        in_specs=[row(D_MODEL), mod_spec, mod_spec, _full_spec((1, D_MODEL)),
                  _full_spec(w_main.shape), _full_spec(w_rot.shape), row(2 * LANES), row(2 * LANES),
                  _full_spec((1, MLA_Q_LORA)), _full_spec((1, MLA_KV_LORA)),
                  _full_spec(wqn.shape), _full_spec(wqr.shape), _full_spec(wqrr.shape), _full_spec(wuk.shape)],
        out_specs=[row(w) for w, _ in widths],
        out_shape=[jax.ShapeDtypeStruct((t, w), dt) for w, dt in widths],
        compiler_params=_cparams(("arbitrary",)),
        name="project",
    )(x, shift, scale, g_mix.reshape(1, -1), w_main, w_rot, cos_t, sin_t,
      g_cq.reshape(1, -1), g_ckv.reshape(1, -1), wqn, wqr, wqrr, wuk)


def _sortable(s):
    b = pltpu.bitcast(s + 0.0, I32)
    return b ^ ((b >> 31) & INT_MAX)


_NT = (((1,), (1,)), ((), ()))


def _fold_lanes(x):
    part = x[:, :LANES]
    for j in range(1, x.shape[1] // LANES):
        part = part + x[:, j * LANES:(j + 1) * LANES]
    return part


def _select_threshold(count, n_sel, rows, idx_bits):
    def bit_step(i, t):
        cand = t ^ lax.shift_left(jnp.int32(1), 31 - i)
        return jnp.where(count(lambda k, c: k >= cand) >= n_sel, cand, t)

    t = lax.fori_loop(0, 32, bit_step, jnp.full((rows, 1), INT_MIN, I32))
    need = n_sel - count(lambda k, c: k > t)
    n_eq = count(lambda k, c: k == t)
    live = t != INT_MIN
    tie = jnp.max(jnp.where((n_eq > need) & live, 1.0, 0.0)) > 0.0

    def tie_search():
        def step(i, j):
            cand = j | lax.shift_left(jnp.int32(1), idx_bits - 1 - i)
            return jnp.where(count(lambda k, c: (k == t) & (c < cand)) <= need, cand, j)
        return lax.fori_loop(0, idx_bits, step, jnp.zeros((rows, 1), I32))

    j = lax.cond(tie, tie_search, lambda: jnp.full((rows, 1), INT_MAX, I32))
    return t, jnp.where(live, j, 0)


def _dsa_prompt_kernel(q_ref, iq_ref, iw_ref, ik_ref, k_ref, v_ref, o_ref,
                       key_ref, qbd_ref, m_ref, l_ref, acc_ref, *, tq, ck, n_sel, idx_bits):
    qb = pl.program_id(0)
    n_chunks = ((qb + 1) * tq + ck - 1) // ck
    row_pos = qb * tq + lax.broadcasted_iota(I32, (tq, 1), 0)
    iw = iw_ref[...]
    iq = iq_ref[...]
    iq_h = [iq[:, h * IDX_DIM:(h + 1) * IDX_DIM] for h in range(IDX_HEADS)]
    iw_h = [iw[:, h:h + 1] for h in range(IDX_HEADS)]

    def cols(off):
        return off + lax.broadcasted_iota(I32, (tq, ck), 1)

    def score_chunk(c, carry):
        off = pl.multiple_of(c * ck, ck)
        ikc = ik_ref[pl.ds(off, ck), :]
        sc = jnp.zeros((tq, ck), F32)
        for h in range(IDX_HEADS):
            r = lax.dot_general(iq_h[h], ikc, _NT, preferred_element_type=F32)
            sc = sc + iw_h[h] * jnp.maximum(r, 0.0)
        key_ref[:, pl.ds(off, ck)] = jnp.where(cols(off) <= row_pos, _sortable(sc), INT_MIN)
        return carry

    lax.fori_loop(0, n_chunks, score_chunk, 0)

    def count(pred):
        def body(c, cnt):
            off = pl.multiple_of(c * ck, ck)
            hit = pred(key_ref[:, pl.ds(off, ck)], cols(off))
            return cnt + _fold_lanes(jnp.where(hit, 1.0, 0.0))
        cnt = lax.fori_loop(0, n_chunks, body, jnp.zeros((tq, LANES), F32))
        return jnp.sum(cnt, axis=1, keepdims=True)

    t, j = _select_threshold(count, n_sel, tq, idx_bits)

    q = q_ref[...]
    lane_g = lax.broadcasted_iota(I32, (tq, KV_W), 1) // HEAD_DIM
    for r in range(DSA_GROUP):
        qr = q[:, r * KV_W:(r + 1) * KV_W]
        for g in range(DSA_KV_HEADS):
            hh = r * DSA_KV_HEADS + g
            qbd_ref[hh * tq:(hh + 1) * tq, :] = jnp.where(lane_g == g, qr, jnp.zeros_like(qr))
    _softmax_init(m_ref, l_ref, acc_ref)

    def attn_chunk(c, carry):
        off = pl.multiple_of(c * ck, ck)
        kc = key_ref[:, pl.ds(off, ck)]
        sel = (kc > t) | ((kc == t) & (cols(off) < j))
        bias = jnp.where(sel, 0.0, NEG_BIG)
        s = lax.dot_general(qbd_ref[...], k_ref[pl.ds(off, ck), :], _NT, preferred_element_type=F32)
        s = s + jnp.concatenate([bias] * DSA_HEADS, axis=0)
        _softmax_step(s, v_ref[pl.ds(off, ck), :], m_ref, l_ref, acc_ref)
        return carry

    lax.fori_loop(0, n_chunks, attn_chunk, 0)
    o_full = acc_ref[...] / l_ref[...]
    for r in range(DSA_GROUP):
        o_r = jnp.zeros((tq, KV_W), F32)
        for g in range(DSA_KV_HEADS):
            hh = r * DSA_KV_HEADS + g
            o_r = o_r + jnp.where(lane_g == g, o_full[hh * tq:(hh + 1) * tq, :], 0.0)
        o_ref[:, r * KV_W:(r + 1) * KV_W] = o_r


def _resident(shape):
    nd = len(shape)
    return pl.BlockSpec(shape, lambda *_: (0,) * nd, pipeline_mode=pl.Buffered(1))


def _dsa_prompt(q16, iq16, iw, ik16, k16, v16, tq=128, ck=512):
    s_len = q16.shape[0]
    ck = min(ck, s_len)
    n_sel = min(TOPK_MAX, s_len // 4)
    rows = DSA_HEADS * tq
    kern = functools.partial(_dsa_prompt_kernel, tq=tq, ck=ck, n_sel=n_sel, idx_bits=int(s_len).bit_length())
    return pl.pallas_call(
        kern,
        grid=(s_len // tq,),
        in_specs=[pl.BlockSpec((tq, DSA_WIDTH), lambda i: (i, 0)),
                  pl.BlockSpec((tq, IDX_HEADS * IDX_DIM), lambda i: (i, 0)),
                  pl.BlockSpec((tq, LANES), lambda i: (i, 0)),
                  _resident(ik16.shape), _resident(k16.shape), _resident(v16.shape)],
        out_specs=pl.BlockSpec((tq, DSA_WIDTH), lambda i: (i, 0)),
        out_shape=jax.ShapeDtypeStruct((s_len, DSA_WIDTH), F32),
        scratch_shapes=[pltpu.VMEM((tq, s_len), I32), pltpu.VMEM((rows, KV_W), BF16),
                        pltpu.VMEM((rows, 1), F32), pltpu.VMEM((rows, 1), F32), pltpu.VMEM((rows, KV_W), F32)],
        compiler_params=_cparams(("arbitrary",)),
        name="dsa_prompt",
    )(q16, iq16, iw, ik16, k16, v16)


def _softmax_step(s, vals, m_ref, l_ref, acc_ref):
    m_prev = m_ref[...]
    m_new = jnp.maximum(m_prev, jnp.max(s, axis=1, keepdims=True))
    alpha = jnp.exp(m_prev - m_new)
    p = jnp.exp(s - m_new)
    l_ref[...] = alpha * l_ref[...] + jnp.sum(p, axis=1, keepdims=True)
    acc_ref[...] = alpha * acc_ref[...] + jnp.dot(p.astype(BF16), vals, preferred_element_type=F32)
    m_ref[...] = m_new


def _softmax_init(m_ref, l_ref, acc_ref):
    m_ref[...] = jnp.full(m_ref.shape, NEG_BIG, F32)
    l_ref[...] = jnp.zeros(l_ref.shape, F32)
    acc_ref[...] = jnp.zeros(acc_ref.shape, F32)


def _mla_prompt_kernel(qcat_ref, kcat_ref, wuv_ref, o_ref, q_ref, m_ref, l_ref, acc_ref, *, tq, ck):
    qb = pl.program_id(0)
    n_chunks = ((qb + 1) * tq + ck - 1) // ck
    row_pos = qb * tq + lax.broadcasted_iota(I32, (tq, 1), 0)
    for hh in range(MLA_HEADS):
        q_ref[hh * tq:(hh + 1) * tq, :] = qcat_ref[:, hh * QCAT_W:(hh + 1) * QCAT_W]
    _softmax_init(m_ref, l_ref, acc_ref)

    def chunk(c, carry):
        off = pl.multiple_of(c * ck, ck)
        kc = kcat_ref[pl.ds(off, ck), :]
        col = off + lax.broadcasted_iota(I32, (tq, ck), 1)
        bias = jnp.where(col <= row_pos, 0.0, NEG_BIG)
        s = lax.dot_general(q_ref[...], kc, _NT, preferred_element_type=F32)
        s = s + jnp.concatenate([bias] * MLA_HEADS, axis=0)
        _softmax_step(s, kc[:, :MLA_KV_LORA], m_ref, l_ref, acc_ref)
        return carry

    lax.fori_loop(0, n_chunks, chunk, 0)
    o_lat = (acc_ref[...] / l_ref[...]).astype(BF16)
    lat = jnp.concatenate([o_lat[hh * tq:(hh + 1) * tq, :] for hh in range(MLA_HEADS)], axis=1)
    o_ref[...] = jnp.dot(lat, wuv_ref[...], preferred_element_type=F32)


def _mla_prompt(qcat, kcat, wuv, tq=128, ck=512):
    s_len = qcat.shape[0]
    ck = min(ck, s_len)
    rows = MLA_HEADS * tq
    return pl.pallas_call(
        functools.partial(_mla_prompt_kernel, tq=tq, ck=ck),
        grid=(s_len // tq,),
        in_specs=[pl.BlockSpec((tq, MLA_HEADS * QCAT_W), lambda i: (i, 0)),
                  _resident(kcat.shape), _resident(wuv.shape)],
        out_specs=pl.BlockSpec((tq, MLA_WIDTH), lambda i: (i, 0)),
        out_shape=jax.ShapeDtypeStruct((s_len, MLA_WIDTH), F32),
        scratch_shapes=[pltpu.VMEM((rows, QCAT_W), BF16), pltpu.VMEM((rows, 1), F32),
                        pltpu.VMEM((rows, 1), F32), pltpu.VMEM((rows, MLA_KV_LORA), F32)],
        compiler_params=_cparams(("arbitrary",)),
        name="mla_prompt",
    )(qcat, kcat, wuv)


PAGES_PER_STEP = 16


def _pad_rows(x):
    return jnp.concatenate([x, jnp.zeros((PAGE_SIZE - x.shape[0], x.shape[1]), x.dtype)], axis=0)


def _tile_rows(x, n):
    return jnp.concatenate([x] * n, axis=0)


def _idx_scores(iq_rows, iw, keys16, t):
    r = lax.dot_general(iq_rows, keys16, _NT, preferred_element_type=F32)
    sc = jnp.zeros((t, keys16.shape[0]), F32)
    for h in range(IDX_HEADS):
        sc = sc + iw[:, h:h + 1] * jnp.maximum(r[h * t:(h + 1) * t, :], 0.0)
    return sc


def _sample_idx_mla_kernel(pt_ref, iq_ref, iw_ref, ikn_ref, qcat_ref, ckvn_ref, krn_ref, wuv_ref, *rest,
                           pg, n_pages, n_sel, idx_bits, t):
    idx_refs, ckv_refs, kr_refs = rest[:pg], rest[pg:2 * pg], rest[2 * pg:3 * pg]
    o_ref, bias_ref = rest[3 * pg:3 * pg + 2]
    key_ref, iqr_ref, qrow_ref, m_ref, l_ref, acc_ref = rest[3 * pg + 2:]
    j = pl.program_id(1)
    past = n_pages * PAGE_SIZE
    span = pg * PAGE_SIZE

    @pl.when(j == 0)
    def _init():
        iq = iq_ref[...]
        for h in range(IDX_HEADS):
            iqr_ref[h * t:(h + 1) * t, :] = iq[:, h * IDX_DIM:(h + 1) * IDX_DIM]
        qc = qcat_ref[...]
        for hh in range(MLA_HEADS):
            qrow_ref[hh * t:(hh + 1) * t, :] = qc[:, hh * QCAT_W:(hh + 1) * QCAT_W]
        _softmax_init(m_ref, l_ref, acc_ref)

    iqr = iqr_ref[...].astype(BF16)
    iw = iw_ref[...]
    qrow = qrow_ref[...]
    q_lat = qrow[:, :MLA_KV_LORA].astype(BF16)
    q_rope = qrow[:, MLA_KV_LORA + IDX_DIM:MLA_KV_LORA + IDX_DIM + MLA_ROPE].astype(BF16)

    def mla_scores(ck16, kr16):
        return (lax.dot_general(q_lat, ck16, _NT, preferred_element_type=F32)
                + lax.dot_general(q_rope, kr16, _NT, preferred_element_type=F32))

    ikc = jnp.concatenate([r[...] for r in idx_refs], axis=0).astype(BF16)
    ckc = jnp.concatenate([r[...] for r in ckv_refs], axis=0).astype(BF16)
    krc = jnp.concatenate([r[...] for r in kr_refs], axis=0).astype(BF16)
    key_ref[:, pl.ds(pl.multiple_of(j * span, span), span)] = _sortable(_idx_scores(iqr, iw, ikc, t))
    _softmax_step(mla_scores(ckc, krc), ckc, m_ref, l_ref, acc_ref)

    @pl.when(j == n_pages // pg - 1)
    def _finish():
        tok = lax.broadcasted_iota(I32, (t, PAGE_SIZE), 0)
        col = lax.broadcasted_iota(I32, (t, PAGE_SIZE), 1)
        vis = col <= tok
        ikn = _pad_rows(ikn_ref[...]).astype(BF16)
        key_ref[:, past:past + PAGE_SIZE] = jnp.where(vis, _sortable(_idx_scores(iqr, iw, ikn, t)), INT_MIN)
        ckn = _pad_rows(ckvn_ref[...]).astype(BF16)
        krn = _pad_rows(krn_ref[...]).astype(BF16)
        s = mla_scores(ckn, krn) + _tile_rows(jnp.where(vis, 0.0, NEG_BIG), MLA_HEADS)
        _softmax_step(s, ckn, m_ref, l_ref, acc_ref)
        o_lat = acc_ref[...] / l_ref[...]
        lat = jnp.concatenate([o_lat[hh * t:(hh + 1) * t, :] for hh in range(MLA_HEADS)], axis=1)
        o_ref[...] = jnp.dot(lat.astype(BF16), wuv_ref[...], preferred_element_type=F32)

        keys = key_ref[...]
        cols = lax.broadcasted_iota(I32, keys.shape, 1)

        def count(pred):
            return jnp.sum(jnp.where(pred(keys, cols), 1.0, 0.0), axis=1, keepdims=True)

        thr, jj = _select_threshold(count, n_sel, t, idx_bits)
        sel = (keys > thr) | ((keys == thr) & (cols < jj))
        bias_ref[...] = jnp.where(sel, 0.0, NEG_BIG)


def _page_specs(width, pg, n_pages, page_off):
    return [pl.BlockSpec((None, PAGE_SIZE, width),
                         functools.partial(lambda s, j, pt, i: (page_off + pt[s * n_pages + j * pg + i], 0, 0), i=i))
            for i in range(pg)]


def _seq_spec(shape):
    return pl.BlockSpec((None,) + tuple(shape[1:]), lambda s, j, pt: (s, 0, 0))


def _sample_idx_mla(pt_flat, iq, iw, ikn, qcat, ckvn, krn, wuv, c_idx, c_ckv, c_kr, n_pages, page_off=0):
    nseq, t, _ = iq.shape
    pg = min(PAGES_PER_STEP, n_pages)
    past = n_pages * PAGE_SIZE
    lp = past + PAGE_SIZE
    n_sel = min(TOPK_MAX, (past + t) // 4)
    rows = MLA_HEADS * t
    kern = functools.partial(_sample_idx_mla_kernel, pg=pg, n_pages=n_pages, n_sel=n_sel,
                             idx_bits=int(past + t).bit_length(), t=t)
    grid_spec = pltpu.PrefetchScalarGridSpec(
        num_scalar_prefetch=1,
        grid=(nseq, n_pages // pg),
        in_specs=[_seq_spec(iq.shape), _seq_spec(iw.shape), _seq_spec(ikn.shape), _seq_spec(qcat.shape),
                  _seq_spec(ckvn.shape), _seq_spec(krn.shape),
                  pl.BlockSpec(wuv.shape, lambda s, j, pt: (0, 0))]
                 + _page_specs(IDX_DIM, pg, n_pages, page_off) + _page_specs(MLA_KV_LORA, pg, n_pages, page_off)
                 + _page_specs(MLA_ROPE, pg, n_pages, page_off),
        out_specs=[pl.BlockSpec((None, t, MLA_WIDTH), lambda s, j, pt: (s, 0, 0)),
                   pl.BlockSpec((None, t, lp), lambda s, j, pt: (s, 0, 0))],
        scratch_shapes=[pltpu.VMEM((t, lp), I32), pltpu.VMEM((IDX_HEADS * t, IDX_DIM), F32),
                        pltpu.VMEM((rows, QCAT_W), F32), pltpu.VMEM((rows, 1), F32),
                        pltpu.VMEM((rows, 1), F32), pltpu.VMEM((rows, MLA_KV_LORA), F32)],
    )
    return pl.pallas_call(
        kern, grid_spec=grid_spec,
        out_shape=[jax.ShapeDtypeStruct((nseq, t, MLA_WIDTH), F32), jax.ShapeDtypeStruct((nseq, t, lp), F32)],
        compiler_params=_cparams(("arbitrary", "arbitrary")),
        name="sample_idx_mla",
    )(pt_flat, iq, iw, ikn, qcat, ckvn, krn, wuv, *([c_idx] * pg), *([c_ckv] * pg), *([c_kr] * pg))


def _sample_dsa_kernel(pt_ref, q_ref, kn_ref, vn_ref, bias_ref, biasn_ref, *rest, pg, n_pages, t):
    k_refs, v_refs = rest[:pg], rest[pg:2 * pg]
    o_ref = rest[2 * pg]
    qbd_ref, m_ref, l_ref, acc_ref = rest[2 * pg + 1:]
    j = pl.program_id(1)
    lane_g = lax.broadcasted_iota(I32, (t, KV_W), 1) // HEAD_DIM

    @pl.when(j == 0)
    def _init():
        q = q_ref[...]
        for r in range(DSA_GROUP):
            qr = q[:, r * KV_W:(r + 1) * KV_W]
            for g in range(DSA_KV_HEADS):
                hh = r * DSA_KV_HEADS + g
                qbd_ref[hh * t:(hh + 1) * t, :] = jnp.where(lane_g == g, qr, 0.0)
        _softmax_init(m_ref, l_ref, acc_ref)

    qbd = qbd_ref[...].astype(BF16)
    kc = jnp.concatenate([r[...] for r in k_refs], axis=0).astype(BF16)
    vc = jnp.concatenate([r[...] for r in v_refs], axis=0).astype(BF16)
    s = lax.dot_general(qbd, kc, _NT, preferred_element_type=F32) + _tile_rows(bias_ref[...], DSA_HEADS)
    _softmax_step(s, vc, m_ref, l_ref, acc_ref)

    @pl.when(j == n_pages // pg - 1)
    def _finish():
        kn = _pad_rows(kn_ref[...]).astype(BF16)
        vn = _pad_rows(vn_ref[...]).astype(BF16)
        s = lax.dot_general(qbd, kn, _NT, preferred_element_type=F32) + _tile_rows(biasn_ref[...], DSA_HEADS)
        _softmax_step(s, vn, m_ref, l_ref, acc_ref)
        o_full = acc_ref[...] / l_ref[...]
        for r in range(DSA_GROUP):
            o_r = jnp.zeros((t, KV_W), F32)
            for g in range(DSA_KV_HEADS):
                hh = r * DSA_KV_HEADS + g
                o_r = o_r + jnp.where(lane_g == g, o_full[hh * t:(hh + 1) * t, :], 0.0)
            o_ref[:, r * KV_W:(r + 1) * KV_W] = o_r


def _sample_dsa(pt_flat, q, kn, vn, bias, c_k, c_v, n_pages, page_off=0):
    nseq, t, _ = q.shape
    pg = min(PAGES_PER_STEP, n_pages)
    rows = DSA_HEADS * t
    span = pg * PAGE_SIZE
    grid_spec = pltpu.PrefetchScalarGridSpec(
        num_scalar_prefetch=1,
        grid=(nseq, n_pages // pg),
        in_specs=[_seq_spec(q.shape), _seq_spec(kn.shape), _seq_spec(vn.shape),
                  pl.BlockSpec((None, t, span), lambda s, j, pt: (s, 0, j)),
                  pl.BlockSpec((None, t, PAGE_SIZE), lambda s, j, pt: (s, 0, n_pages))]
                 + _page_specs(KV_W, pg, n_pages, page_off) + _page_specs(KV_W, pg, n_pages, page_off),
        out_specs=pl.BlockSpec((None, t, DSA_WIDTH), lambda s, j, pt: (s, 0, 0)),
        scratch_shapes=[pltpu.VMEM((rows, KV_W), F32), pltpu.VMEM((rows, 1), F32),
                        pltpu.VMEM((rows, 1), F32), pltpu.VMEM((rows, KV_W), F32)],
    )
    return pl.pallas_call(
        functools.partial(_sample_dsa_kernel, pg=pg, n_pages=n_pages, t=t), grid_spec=grid_spec,
        out_shape=jax.ShapeDtypeStruct((nseq, t, DSA_WIDTH), F32),
        compiler_params=_cparams(("arbitrary", "arbitrary")),
        name="sample_dsa",
    )(pt_flat, q, kn, vn, bias, bias, *([c_k] * pg), *([c_v] * pg))


def _merge_kernel(x_ref, od_ref, om_ref, gate_ref, gd_ref, gm_ref, wo_ref, o_ref):
    a = jnp.concatenate([_rms(od_ref[...]) * gd_ref[...], _rms(om_ref[...]) * gm_ref[...]], axis=1)
    y = jnp.dot(a.astype(BF16), wo_ref[...], preferred_element_type=F32)
    o_ref[...] = x_ref[...] + gate_ref[...] * y


def _mod_spec(a, tm):
    if a.shape[0] == 1:
        return pl.BlockSpec((1, a.shape[1]), lambda i, *_: (0, 0))
    return pl.BlockSpec((tm, a.shape[1]), lambda i, *_: (i, 0))


def _merge(x, o_dsa, o_mla, gate, g_dsa, g_mla, w_out16, tm=256):
    t = x.shape[0]
    tm = min(tm, t)

    def row(w):
        return pl.BlockSpec((tm, w), lambda i: (i, 0))

    return pl.pallas_call(
        _merge_kernel,
        grid=(t // tm,),
        in_specs=[row(D_MODEL), row(DSA_WIDTH), row(MLA_WIDTH), _mod_spec(gate, tm),
                  _full_spec((1, DSA_WIDTH)), _full_spec((1, MLA_WIDTH)), _full_spec(w_out16.shape)],
        out_specs=row(D_MODEL),
        out_shape=jax.ShapeDtypeStruct((t, D_MODEL), F32),
        compiler_params=_cparams(("arbitrary",)),
        name="merge",
    )(x, o_dsa, o_mla, gate, g_dsa.reshape(1, -1), g_mla.reshape(1, -1), w_out16)


EXPERT_CHUNK = 2048
I_PER_CHUNK = EXPERT_CHUNK // PEER_NKEYS


def _erf(x):
    return lax.erf(x)


def _extract_top(cur, n):
    vals = []
    for _ in range(n):
        mx = jnp.max(cur, axis=0, keepdims=True)
        vals.append(mx)
        cur = jnp.where(cur == mx, -jnp.inf, cur)
    return vals


def _peer_kernel(x1_ref, shift_ref, scale_ref, gate_ref, gffn_ref, gfin_ref, wpq_ref, keys_ref, u_ref, vt_ref,
                 y_ref, ht_ref, qry_ref, s0_ref, s1_ref, e0_ref, e1_ref, tau_ref, act_ref, ga_ref, acc_ref,
                 *, tm, final_norm):
    c = pl.program_id(1)

    @pl.when(c == 0)
    def _prep():
        h = _rms(x1_ref[...]) * gffn_ref[...] * (1.0 + scale_ref[...]) + shift_ref[...]
        ht_ref[...] = h.T.astype(BF16)
        qry_ref[...] = jnp.dot(wpq_ref[...], ht_ref[...], preferred_element_type=F32)
        acc_ref[...] = jnp.zeros(acc_ref.shape, F32)

        def head(hh, carry):
            tops = []
            for p, s_ref in ((0, s0_ref), (1, s1_ref)):
                blk = pl.multiple_of((hh * 2 + p) * PEER_HALF, PEER_HALF)
                sub = jnp.dot(keys_ref[pl.ds(blk, PEER_NKEYS), :], qry_ref[pl.ds(blk, PEER_HALF), :],
                              preferred_element_type=F32, precision=lax.Precision.HIGHEST)
                s_ref[hh] = sub
                tops.append(_extract_top(sub, PEER_TOPK))
            top0, top1 = tops
            t1 = jnp.concatenate(top1, axis=0)
            cand = jnp.concatenate([top0[a] + t1 for a in range(PEER_TOPK)], axis=0)
            tau = _extract_top(cand, PEER_TOPK)[-1]
            cmax = top0[0] + top1[0]
            z = jnp.sum(jnp.where(cand >= tau, jnp.exp(cand - cmax), 0.0), axis=0, keepdims=True)
            e0_ref[hh] = jnp.exp(s0_ref[hh] - top0[0]) / z
            e1_ref[hh] = jnp.exp(s1_ref[hh] - top1[0])
            tau_ref[pl.ds(hh, 1), :] = tau
            return carry

        lax.fori_loop(0, PEER_HEADS, head, 0)

    a = jnp.dot(u_ref[...], ht_ref[...], preferred_element_type=F32)
    act_ref[...] = 0.5 * a * (1.0 + _erf(a * (2.0 ** -0.5)))

    def i_slab(il, carry):
        i = c * I_PER_CHUNK + il
        g = jnp.zeros((PEER_NKEYS, tm), F32)
        for hh in range(PEER_HEADS):
            tsum = s0_ref[hh, pl.ds(i, 1), :] + s1_ref[hh]
            hit = tsum >= tau_ref[hh:hh + 1, :]
            g = g + jnp.where(hit, e1_ref[hh], 0.0) * e0_ref[hh, pl.ds(i, 1), :]
        rows = pl.ds(pl.multiple_of(il * PEER_NKEYS, PEER_NKEYS), PEER_NKEYS)
        ga_ref[rows, :] = (g * act_ref[rows, :]).astype(BF16)
        return carry

    lax.fori_loop(0, I_PER_CHUNK, i_slab, 0)
    acc_ref[...] += jnp.dot(vt_ref[...], ga_ref[...], preferred_element_type=F32)

    @pl.when(c == pl.num_programs(1) - 1)
    def _finish():
        x2 = x1_ref[...] + gate_ref[...] * acc_ref[...].T
        y_ref[...] = _rms(x2) * gfin_ref[...] if final_norm else x2


def _peer_final(x1, shift, scale, gate, g_ffn, g_final, wpq_t16, keys2d, u16, vt16, final_norm, tm=256):
    t = x1.shape[0]
    tm = min(tm, t)
    n_chunks = N_EXPERTS // EXPERT_CHUNK
    hshape = (PEER_HEADS, PEER_NKEYS, tm)
    return pl.pallas_call(
        functools.partial(_peer_kernel, tm=tm, final_norm=final_norm),
        grid=(t // tm, n_chunks),
        in_specs=[pl.BlockSpec((tm, D_MODEL), lambda i, c: (i, 0)),
                  _mod_spec(shift, tm), _mod_spec(scale, tm), _mod_spec(gate, tm),
                  _full_spec((1, D_MODEL)), _full_spec((1, D_MODEL)),
                  _resident(wpq_t16.shape), _resident(keys2d.shape),
                  pl.BlockSpec((EXPERT_CHUNK, D_MODEL), lambda i, c: (c, 0)),
                  pl.BlockSpec((D_MODEL, EXPERT_CHUNK), lambda i, c: (0, c))],
        out_specs=pl.BlockSpec((tm, D_MODEL), lambda i, c: (i, 0)),
        out_shape=jax.ShapeDtypeStruct((t, D_MODEL), F32),
        scratch_shapes=[pltpu.VMEM((D_MODEL, tm), BF16), pltpu.VMEM((2 * PEER_HEADS * PEER_HALF, tm), F32),
                        pltpu.VMEM(hshape, F32), pltpu.VMEM(hshape, F32), pltpu.VMEM(hshape, F32),
                        pltpu.VMEM(hshape, F32), pltpu.VMEM((PEER_HEADS, tm), F32),
                        pltpu.VMEM((EXPERT_CHUNK, tm), F32), pltpu.VMEM((EXPERT_CHUNK, tm), BF16),
                        pltpu.VMEM((D_MODEL, tm), F32)],
        compiler_params=_cparams(("arbitrary", "arbitrary")),
        name="peer_final",
    )(x1, shift, scale, gate, g_ffn.reshape(1, -1), g_final.reshape(1, -1), wpq_t16, keys2d, u16, vt16)


def _mod_rows(m, k, n_rep):
    rows = m[:, k * D_MODEL:(k + 1) * D_MODEL]
    return rows if rows.shape[0] == 1 else jnp.repeat(rows, n_rep, axis=0)


def _perm_heads(a):
    tail = a.shape[1:]
    return a.reshape((DSA_KV_HEADS, DSA_GROUP, HEAD_DIM) + tail).swapaxes(0, 1).reshape((DSA_WIDTH,) + tail)


def kernel(x_prompt, x_sample, cache_dsa_k, cache_dsa_v, cache_idx_k, cache_mla_ckv, cache_mla_krope, page_table, c_prompt, c_sample, w_mod, b_mod, g_mix, g_ffn, w_in, g_cq, g_ckv, w_uq, w_uk, w_uv, g_out_dsa, g_out_mla, w_out, w_pq, peer_keys, peer_u, peer_v, g_final):
    depth = w_mod.shape[0]
    b, s_len, _ = x_prompt.shape
    nseq, t, _ = x_sample.shape
    n_pages = page_table.shape[1]
    past = n_pages * PAGE_SIZE
    n_pool = cache_dsa_k.shape[1]
    xp = x_prompt.reshape(b * s_len, D_MODEL)
    xs = x_sample.reshape(nseq * t, D_MODEL)
    cos_p, sin_p = _rope_tables(jnp.tile(jnp.arange(s_len), b))
    cos_s, sin_s = _rope_tables(jnp.tile(past + jnp.arange(t), nseq))
    pt_flat = page_table.reshape(-1)
    c_k = cache_dsa_k.reshape(depth * n_pool, PAGE_SIZE, KV_W)
    c_v = cache_dsa_v.reshape(depth * n_pool, PAGE_SIZE, KV_W)
    c_idx = cache_idx_k.reshape(depth * n_pool, PAGE_SIZE, IDX_DIM)
    c_ckv = cache_mla_ckv.reshape(depth * n_pool, PAGE_SIZE, MLA_KV_LORA)
    c_kr = cache_mla_krope.reshape(depth * n_pool, PAGE_SIZE, MLA_ROPE)
    new_p = [[] for _ in range(5)]
    new_s = [[] for _ in range(5)]

    for l in range(depth):
        last = l == depth - 1
        mod = _adaln(jnp.concatenate([c_prompt, c_sample], axis=0), w_mod[l], b_mod[l])
        mp, ms = mod[:b], mod[b:]
        wqn, wqr, wqrr, wuk, wuv = _pack_mla_weights(w_uq[l], w_uk[l], w_uv[l])
        packed = _pack_in_weights(w_in[l]) + (wqn, wqr, wqrr, wuk)
        g_dsa = _perm_heads(g_out_dsa[l])
        w_out16 = jnp.concatenate([_perm_heads(w_out[l][:DSA_WIDTH]), w_out[l][DSA_WIDTH:]], axis=0).astype(BF16)
        peer_w = (w_pq[l].T.astype(BF16), peer_keys[l].reshape(2 * PEER_HEADS * PEER_NKEYS, PEER_HALF),
                  peer_u[l].astype(BF16), peer_v[l].T.astype(BF16))

        (k32, v32, ik32, ckv32, kr32, q16, iq16, iw, k16, v16, ik16, kcat, qcat) = _project(
            xp, _mod_rows(mp, 0, s_len), _mod_rows(mp, 1, s_len), g_mix[l], packed, cos_p, sin_p, g_cq[l], g_ckv[l])
        o_dsa, o_mla = [], []
        for bi in range(b):
            sl = slice(bi * s_len, (bi + 1) * s_len)
            o_dsa.append(_dsa_prompt(q16[sl], iq16[sl], iw[sl], ik16[sl], k16[sl], v16[sl]))
            o_mla.append(_mla_prompt(qcat[sl], kcat[sl], wuv))
        o_dsa = o_dsa[0] if b == 1 else jnp.concatenate(o_dsa, axis=0)
        o_mla = o_mla[0] if b == 1 else jnp.concatenate(o_mla, axis=0)
        x1 = _merge(xp, o_dsa, o_mla, _mod_rows(mp, 2, s_len), g_dsa, g_out_mla[l], w_out16)
        xp = _peer_final(x1, _mod_rows(mp, 3, s_len), _mod_rows(mp, 4, s_len), _mod_rows(mp, 5, s_len),
                         g_ffn[l], g_final, *peer_w, final_norm=last)
        for lst, (val, shp) in zip(new_p, ((k32, (DSA_KV_HEADS, HEAD_DIM)), (v32, (DSA_KV_HEADS, HEAD_DIM)),
                                           (ik32, (IDX_DIM,)), (ckv32, (MLA_KV_LORA,)), (kr32, (MLA_ROPE,)))):
            lst.append(val.reshape((b, s_len) + shp))

        ms_tok = [jnp.repeat(ms[:, k * D_MODEL:(k + 1) * D_MODEL], t, axis=0) for k in range(6)]
        (k32, v32, ik32, ckv32, kr32, q16, iq16, iw, k16, v16, ik16, kcat, qcat) = _project(
            xs, ms_tok[0], ms_tok[1], g_mix[l], packed, cos_s, sin_s, g_cq[l], g_ckv[l], tm=min(256, nseq * t))

        def seq3(a):
            return a.reshape(nseq, t, a.shape[-1]).astype(F32)

        o_mla, bias = _sample_idx_mla(pt_flat, seq3(iq16), seq3(iw), seq3(ik32), seq3(qcat), seq3(ckv32),
                                      seq3(kr32), wuv, c_idx, c_ckv, c_kr, n_pages, page_off=l * n_pool)
        o_dsa = _sample_dsa(pt_flat, seq3(q16), seq3(k32), seq3(v32), bias, c_k, c_v, n_pages,
                            page_off=l * n_pool)
        x1 = _merge(xs, o_dsa.reshape(nseq * t, DSA_WIDTH), o_mla.reshape(nseq * t, MLA_WIDTH), ms_tok[2],
                    g_dsa, g_out_mla[l], w_out16)
        xs = _peer_final(x1, ms_tok[3], ms_tok[4], ms_tok[5], g_ffn[l], g_final, *peer_w, final_norm=last)
        for lst, (val, shp) in zip(new_s, ((k32, (DSA_KV_HEADS, HEAD_DIM)), (v32, (DSA_KV_HEADS, HEAD_DIM)),
                                           (ik32, (IDX_DIM,)), (ckv32, (MLA_KV_LORA,)), (kr32, (MLA_ROPE,)))):
            lst.append(val.reshape((nseq, t) + shp))

    sp = [jnp.stack(a, axis=0) for a in new_p]
    ss = [jnp.stack(a, axis=0) for a in new_s]
    return (xp.reshape(b, s_len, D_MODEL), xs.reshape(nseq, t, D_MODEL),
            sp[0], sp[1], sp[2], sp[3], sp[4], ss[0], ss[1], ss[2], ss[3], ss[4])
```

```python
import functools

import numpy as np
import jax
import jax.numpy as jnp
from jax import lax
from jax.experimental import pallas as pl
from jax.experimental.pallas import tpu as pltpu

F32 = jnp.float32
BF16 = jnp.bfloat16
I32 = jnp.int32

D_MODEL = 1024
HEAD_DIM = 64
DSA_WIDTH = 512
DSA_HEADS = 8
DSA_KV_HEADS = 4
DSA_GROUP = 2
KV_W = DSA_KV_HEADS * HEAD_DIM
IDX_HEADS = 4
IDX_DIM = 64
TOPK_MAX = 256
MLA_V = 64
MLA_WIDTH = 512
MLA_HEADS = 8
MLA_NOPE = 64
MLA_ROPE = 32
MLA_Q_LORA = 384
MLA_KV_LORA = 256
MLA_SCALE = (MLA_NOPE + MLA_ROPE) ** -0.5
PEER_HEADS = 8
PEER_NKEYS = 128
N_EXPERTS = PEER_NKEYS * PEER_NKEYS
PEER_HALF = 128
PEER_TOPK = 16
PAGE_SIZE = 128
ROPE_THETA = 10000.0
EPS = 1e-6
IN_SPLITS = (DSA_WIDTH, KV_W, KV_W, IDX_HEADS * IDX_DIM, IDX_HEADS, IDX_DIM,
             MLA_Q_LORA, MLA_KV_LORA, MLA_ROPE)

LANES = 128
SUBLANES = 8
VMEM_LIMIT = 56 * 1024 * 1024

C_Q, C_K, C_IQ, C_D, C_V, C_CKV, C_CQ, C_IW, C_END = 0, 512, 768, 1024, 1152, 1408, 1664, 2048, 2176
ROT_W = C_V
QCAT_W = 384
INT_MIN = -2147483648
INT_MAX = 2147483647
NEG_BIG = -1e30


def _cparams(sem):
    return pltpu.CompilerParams(dimension_semantics=sem, vmem_limit_bytes=VMEM_LIMIT)


def _full_spec(shape):
    nd = len(shape)
    return pl.BlockSpec(shape, lambda *_: (0,) * nd)


def _adaln_kernel(c_ref, w_ref, b_ref, o_ref):
    c = c_ref[...]
    a = c / (1.0 + jnp.exp(-c))
    o_ref[...] = jnp.dot(a, w_ref[...], preferred_element_type=F32,
                         precision=lax.Precision.HIGHEST) + b_ref[...]


def _adaln(c, w_mod, b_mod):
    n = c.shape[0]
    n_pad = -(-n // SUBLANES) * SUBLANES
    cp = jnp.pad(c, ((0, n_pad - n), (0, 0)))
    tn = 1536
    out = pl.pallas_call(
        _adaln_kernel,
        grid=(w_mod.shape[1] // tn,),
        in_specs=[pl.BlockSpec((n_pad, D_MODEL), lambda j: (0, 0)),
                  pl.BlockSpec((D_MODEL, tn), lambda j: (0, j)),
                  pl.BlockSpec((1, tn), lambda j: (0, j))],
        out_specs=pl.BlockSpec((n_pad, tn), lambda j: (0, j)),
        out_shape=jax.ShapeDtypeStruct((n_pad, w_mod.shape[1]), F32),
        compiler_params=_cparams(("arbitrary",)),
        name="adaln",
    )(cp, w_mod, b_mod.reshape(1, -1))
    return out[:n]


def _swap_halves(w, hd):
    d, n = w.shape
    w4 = w.reshape(d, n // hd, 2, hd // 2)
    return w4[:, :, ::-1, :].reshape(d, n)


def _pack_in_weights(w_in):
    cuts = [int(v) for v in np.cumsum(IN_SPLITS)[:-1]]
    wq, wk, wv, wiq, wiw, wik, wcq, wckv, wkr = jnp.split(w_in, cuts, axis=1)
    wq = wq.reshape(D_MODEL, DSA_KV_HEADS, DSA_GROUP, HEAD_DIM).transpose(0, 2, 1, 3).reshape(D_MODEL, DSA_WIDTH)
    pad32 = jnp.zeros((D_MODEL, 32), F32)
    w_d = jnp.concatenate([wik, wkr, pad32], axis=1)
    w_d_rot = jnp.concatenate([_swap_halves(wik, IDX_DIM), _swap_halves(wkr, MLA_ROPE), pad32], axis=1)
    wiw_p = jnp.pad(wiw, ((0, 0), (0, LANES - IDX_HEADS)))
    w_main = jnp.concatenate([wq, wk, wiq, w_d, wv, wckv, wcq, wiw_p], axis=1)
    w_rot = jnp.concatenate([_swap_halves(wq, HEAD_DIM), _swap_halves(wk, HEAD_DIM),
                             _swap_halves(wiq, IDX_DIM), w_d_rot], axis=1)
    return w_main.astype(BF16), w_rot.astype(BF16)


def _pack_mla_weights(w_uq, w_uk, w_uv):
    w_n = w_uq[:, :, :MLA_NOPE].reshape(MLA_Q_LORA, MLA_HEADS * MLA_NOPE)
    w_r = w_uq[:, :, MLA_NOPE:]
    w_r_rot = jnp.concatenate([w_r[..., MLA_ROPE // 2:], w_r[..., :MLA_ROPE // 2]], axis=-1)

    def place(w):
        z = jnp.zeros((MLA_Q_LORA, MLA_HEADS, LANES), F32)
        return z.at[:, :, IDX_DIM:IDX_DIM + MLA_ROPE].set(w).reshape(MLA_Q_LORA, MLA_HEADS * LANES)

    eye = jnp.eye(MLA_HEADS, dtype=F32)
    uk = jnp.einsum('chd,hk->hdkc', w_uk, eye).reshape(MLA_HEADS * MLA_NOPE, MLA_HEADS * MLA_KV_LORA)
    uv = jnp.einsum('chd,hk->hckd', w_uv, eye).reshape(MLA_HEADS * MLA_KV_LORA, MLA_HEADS * MLA_V)
    return (w_n.astype(BF16), place(w_r).astype(BF16), place(w_r_rot).astype(BF16),
            uk.astype(BF16), uv.astype(BF16))


def _rope_tables(pos):
    pos = pos.astype(F32)[:, None]

    def cs(dim):
        half = dim // 2
        inv = ROPE_THETA ** (-jnp.arange(half, dtype=F32) / half)
        ang = pos * inv[None, :]
        c, s = jnp.cos(ang), jnp.sin(ang)
        return jnp.concatenate([c, c], axis=1), jnp.concatenate([-s, s], axis=1)

    c64, s64 = cs(HEAD_DIM)
    c32, s32 = cs(MLA_ROPE)
    z = jnp.zeros((pos.shape[0], 32), F32)
    return (jnp.concatenate([c64, c64, c64, c32, z], axis=1),
            jnp.concatenate([s64, s64, s64, s32, z], axis=1))


def _rms(x):
    return x * lax.rsqrt(jnp.mean(x * x, axis=-1, keepdims=True) + EPS)


def _proj_kernel(x_ref, shift_ref, scale_ref, gmix_ref, wm_ref, wr_ref, cos_ref, sin_ref,
                 gcq_ref, gckv_ref, wqn_ref, wqr_ref, wqrr_ref, wuk_ref,
                 k32_ref, v32_ref, ik32_ref, ckv32_ref, kr32_ref,
                 q16_ref, iq16_ref, iw_ref, v16_ref, kcat_ref, qcat_ref, kt16_ref, ikt16_ref, kcatt_ref):
    x = x_ref[...]
    h = _rms(x) * gmix_ref[...] * (1.0 + scale_ref[...]) + shift_ref[...]
    hb = h.astype(BF16)
    main = jnp.dot(hb, wm_ref[...], preferred_element_type=F32)
    rot = jnp.dot(hb, wr_ref[...], preferred_element_type=F32)
    cos = cos_ref[...]
    sin = sin_ref[...]
    cos_h, sin_h = cos[:, :LANES], sin[:, :LANES]
    cos_d, sin_d = cos[:, LANES:], sin[:, LANES:]

    def rope(lo, hi):
        reps = (hi - lo) // LANES
        c = jnp.concatenate([cos_h] * reps, axis=1)
        s = jnp.concatenate([sin_h] * reps, axis=1)
        return main[:, lo:hi] * c + rot[:, lo:hi] * s

    q = rope(C_Q, C_K)
    k = rope(C_K, C_IQ)
    iq = rope(C_IQ, C_D)
    sec_d = main[:, C_D:C_V] * cos_d + rot[:, C_D:C_V] * sin_d
    v = main[:, C_V:C_CKV]
    ckv = _rms(main[:, C_CKV:C_CQ]) * gckv_ref[...]
    cqn = (_rms(main[:, C_CQ:C_IW]) * gcq_ref[...]).astype(BF16)

    k32_ref[...] = k
    v32_ref[...] = v
    ik32_ref[...] = sec_d[:, :IDX_DIM]
    ckv32_ref[...] = ckv
    kr32_ref[...] = sec_d[:, IDX_DIM:IDX_DIM + MLA_ROPE]
    q16_ref[...] = (q * HEAD_DIM ** -0.5).astype(BF16)
    iq16_ref[...] = iq.astype(BF16)
    iw_ref[...] = main[:, C_IW:C_END] * (IDX_HEADS ** -0.5 * IDX_DIM ** -0.5)
    v16_ref[...] = v.astype(BF16)
    lane = lax.broadcasted_iota(I32, sec_d.shape, 1)
    kr_blk = jnp.where((lane >= IDX_DIM) & (lane < IDX_DIM + MLA_ROPE), sec_d, 0.0)
    kcat_ref[...] = jnp.concatenate([ckv, kr_blk], axis=1).astype(BF16)
    kt16_ref[...] = k.T.astype(BF16)
    ikt16_ref[...] = sec_d.T[:IDX_DIM, :].astype(BF16)
    kcatt_ref[...] = jnp.concatenate([ckv.T, kr_blk.T], axis=0).astype(BF16)

    qn = jnp.dot(cqn, wqn_ref[...], preferred_element_type=F32).astype(BF16)
    qlat = jnp.dot(qn, wuk_ref[...], preferred_element_type=F32) * MLA_SCALE
    qr_m = jnp.dot(cqn, wqr_ref[...], preferred_element_type=F32)
    qr_r = jnp.dot(cqn, wqrr_ref[...], preferred_element_type=F32)
    cd = jnp.concatenate([cos_d] * MLA_HEADS, axis=1)
    sd = jnp.concatenate([sin_d] * MLA_HEADS, axis=1)
    qr = (qr_m * cd + qr_r * sd) * MLA_SCALE
    for hh in range(MLA_HEADS):
        qcat_ref[:, hh * QCAT_W:hh * QCAT_W + MLA_KV_LORA] = (
            qlat[:, hh * MLA_KV_LORA:(hh + 1) * MLA_KV_LORA].astype(BF16))
        qcat_ref[:, hh * QCAT_W + MLA_KV_LORA:(hh + 1) * QCAT_W] = (
            qr[:, hh * LANES:(hh + 1) * LANES].astype(BF16))


def _project(x, shift, scale, g_mix, packed, cos_t, sin_t, g_cq, g_ckv, tm=256):
    t = x.shape[0]
    w_main, w_rot, wqn, wqr, wqrr, wuk = packed
    per_tok = shift.shape[0] != 1
    mod_spec = pl.BlockSpec((tm, D_MODEL), lambda i: (i, 0)) if per_tok else _full_spec((1, D_MODEL))

    def row(w):
        return pl.BlockSpec((tm, w), lambda i: (i, 0))

    widths = [(KV_W, F32), (KV_W, F32), (IDX_DIM, F32), (MLA_KV_LORA, F32), (MLA_ROPE, F32),
              (DSA_WIDTH, BF16), (IDX_HEADS * IDX_DIM, BF16), (LANES, F32), (KV_W, BF16),
              (QCAT_W, BF16), (MLA_HEADS * QCAT_W, BF16)]
    heights = [KV_W, IDX_DIM, QCAT_W]
    return pl.pallas_call(
        _proj_kernel,
        grid=(t // tm,),
        in_specs=[row(D_MODEL), mod_spec, mod_spec, _full_spec((1, D_MODEL)),
                  _full_spec(w_main.shape), _full_spec(w_rot.shape), row(2 * LANES), row(2 * LANES),
                  _full_spec((1, MLA_Q_LORA)), _full_spec((1, MLA_KV_LORA)),
                  _full_spec(wqn.shape), _full_spec(wqr.shape), _full_spec(wqrr.shape), _full_spec(wuk.shape)],
        out_specs=[row(w) for w, _ in widths] + [pl.BlockSpec((h, tm), lambda i: (0, i)) for h in heights],
        out_shape=[jax.ShapeDtypeStruct((t, w), dt) for w, dt in widths]
                  + [jax.ShapeDtypeStruct((h, t), BF16) for h in heights],
        compiler_params=_cparams(("arbitrary",)),
        name="project",
    )(x, shift, scale, g_mix.reshape(1, -1), w_main, w_rot, cos_t, sin_t,
      g_cq.reshape(1, -1), g_ckv.reshape(1, -1), wqn, wqr, wqrr, wuk)


def _sortable(s):
    b = pltpu.bitcast(s + 0.0, I32)
    return b ^ ((b >> 31) & INT_MAX)


_NT = (((1,), (1,)), ((), ()))


def _fold_lanes(x):
    part = x[:, :LANES]
    for j in range(1, x.shape[1] // LANES):
        part = part + x[:, j * LANES:(j + 1) * LANES]
    return part


def _select_threshold(count, n_sel, rows, idx_bits):
    def bit_step(i, t):
        cand = t ^ lax.shift_left(jnp.int32(1), 31 - i)
        return jnp.where(count(lambda k, c: k >= cand) >= n_sel, cand, t)

    t = lax.fori_loop(0, 32, bit_step, jnp.full((rows, 1), INT_MIN, I32))
    need = n_sel - count(lambda k, c: k > t)
    n_eq = count(lambda k, c: k == t)
    live = t != INT_MIN
    tie = jnp.max(jnp.where((n_eq > need) & live, 1.0, 0.0)) > 0.0

    def tie_search():
        def step(i, j):
            cand = j | lax.shift_left(jnp.int32(1), idx_bits - 1 - i)
            return jnp.where(count(lambda k, c: (k == t) & (c < cand)) <= need, cand, j)
        return lax.fori_loop(0, idx_bits, step, jnp.zeros((rows, 1), I32))

    j = lax.cond(tie, tie_search, lambda: jnp.full((rows, 1), INT_MAX, I32))
    return t, jnp.where(live, j, 0)


def _pipelined_chunks(n_chunks, scores, consume):
    scores(0, 0)

    def pair(i, carry):
        c = 2 * i
        scores(c + 1, 1)
        consume(c, 0)
        scores(jnp.minimum(c + 2, n_chunks - 1), 0)
        consume(c + 1, 1)
        return carry

    lax.fori_loop(0, n_chunks // 2, pair, 0)

    @pl.when(n_chunks % 2 == 1)
    def _():
        consume(n_chunks - 1, 0)


def _dsa_prompt_kernel(q_ref, iq_ref, iw_ref, ikt_ref, kt_ref, v_ref, o_ref,
                       key_ref, qbd_ref, s0_ref, s1_ref, m_ref, l_ref, acc_ref, *, tq, ck, cka, n_sel, idx_bits):
    qb = pl.program_id(0)
    n_att = ((qb + 1) * tq + cka - 1) // cka
    n_chunks = n_att * (cka // ck)
    row_pos = qb * tq + lax.broadcasted_iota(I32, (tq, 1), 0)
    iw = iw_ref[...]
    iq = iq_ref[...]
    iq_h = [iq[:, h * IDX_DIM:(h + 1) * IDX_DIM] for h in range(IDX_HEADS)]
    iw_h = [iw[:, h:h + 1] for h in range(IDX_HEADS)]

    def cols(off):
        return off + lax.broadcasted_iota(I32, (tq, ck), 1)

    def score_chunk(c, carry):
        off = pl.multiple_of(c * ck, ck)
        ikc = ikt_ref[:, pl.ds(off, ck)]
        sc = jnp.zeros((tq, ck), F32)
        for h in range(IDX_HEADS):
            r = jnp.dot(iq_h[h], ikc, preferred_element_type=F32)
            sc = sc + iw_h[h] * jnp.maximum(r, 0.0)
        key_ref[:, pl.ds(off, ck)] = jnp.where(cols(off) <= row_pos, _sortable(sc), INT_MIN)
        return carry

    lax.fori_loop(0, n_chunks, score_chunk, 0)

    def count(pred):
        def body(c, cnt):
            off = pl.multiple_of(c * ck, ck)
            hit = pred(key_ref[:, pl.ds(off, ck)], cols(off))
            return cnt + _fold_lanes(jnp.where(hit, 1.0, 0.0))
        cnt = lax.fori_loop(0, n_chunks, body, jnp.zeros((tq, LANES), F32))
        return jnp.sum(cnt, axis=1, keepdims=True)

    t, j = _select_threshold(count, n_sel, tq, idx_bits)

    q = q_ref[...]
    lane_g = lax.broadcasted_iota(I32, (tq, KV_W), 1) // HEAD_DIM
    for r in range(DSA_GROUP):
        qr = q[:, r * KV_W:(r + 1) * KV_W]
        for g in range(DSA_KV_HEADS):
            hh = r * DSA_KV_HEADS + g
            qbd_ref[hh * tq:(hh + 1) * tq, :] = jnp.where(lane_g == g, qr, jnp.zeros_like(qr))
    _softmax_init(m_ref, l_ref, acc_ref)

    s_bufs = (s0_ref, s1_ref)

    def scores(c, slot):
        off = pl.multiple_of(c * cka, cka)
        s_bufs[slot][...] = jnp.dot(qbd_ref[...], kt_ref[:, pl.ds(off, cka)], preferred_element_type=F32)

    def consume(c, slot):
        off = pl.multiple_of(c * cka, cka)
        kc = key_ref[:, pl.ds(off, cka)]
        col = off + lax.broadcasted_iota(I32, (tq, cka), 1)
        sel = (kc > t) | ((kc == t) & (col < j))
        bias = jnp.where(sel, 0.0, NEG_BIG)
        s = s_bufs[slot][...] + jnp.concatenate([bias] * DSA_HEADS, axis=0)
        _softmax_step(s, v_ref[pl.ds(off, cka), :], m_ref, l_ref, acc_ref)

    _pipelined_chunks(n_att, scores, consume)
    o_full = acc_ref[...] / l_ref[...]
    for r in range(DSA_GROUP):
        o_r = jnp.zeros((tq, KV_W), F32)
        for g in range(DSA_KV_HEADS):
            hh = r * DSA_KV_HEADS + g
            o_r = o_r + jnp.where(lane_g == g, o_full[hh * tq:(hh + 1) * tq, :], 0.0)
        o_ref[:, r * KV_W:(r + 1) * KV_W] = o_r


def _resident(shape):
    nd = len(shape)
    return pl.BlockSpec(shape, lambda *_: (0,) * nd, pipeline_mode=pl.Buffered(1))


def _dsa_prompt(q16, iq16, iw, ikt16, kt16, v16, tq=128, ck=512, cka=1024):
    s_len = q16.shape[0]
    cka = min(cka, s_len)
    ck = min(ck, cka)
    n_sel = min(TOPK_MAX, s_len // 4)
    rows = DSA_HEADS * tq
    kern = functools.partial(_dsa_prompt_kernel, tq=tq, ck=ck, cka=cka, n_sel=n_sel,
                             idx_bits=int(s_len).bit_length())
    return pl.pallas_call(
        kern,
        grid=(s_len // tq,),
        in_specs=[pl.BlockSpec((tq, DSA_WIDTH), lambda i: (i, 0)),
                  pl.BlockSpec((tq, IDX_HEADS * IDX_DIM), lambda i: (i, 0)),
                  pl.BlockSpec((tq, LANES), lambda i: (i, 0)),
                  _resident(ikt16.shape), _resident(kt16.shape), _resident(v16.shape)],
        out_specs=pl.BlockSpec((tq, DSA_WIDTH), lambda i: (i, 0)),
        out_shape=jax.ShapeDtypeStruct((s_len, DSA_WIDTH), F32),
        scratch_shapes=[pltpu.VMEM((tq, s_len), I32), pltpu.VMEM((rows, KV_W), BF16),
                        pltpu.VMEM((rows, cka), F32), pltpu.VMEM((rows, cka), F32),
                        pltpu.VMEM((rows, 1), F32), pltpu.VMEM((rows, 1), F32), pltpu.VMEM((rows, KV_W), F32)],
        compiler_params=_cparams(("arbitrary",)),
        name="dsa_prompt",
    )(q16, iq16, iw, ikt16, kt16, v16)


def _softmax_step(s, vals, m_ref, l_ref, acc_ref, vals_keys_on_lanes=False):
    m_prev = m_ref[...]
    m_new = jnp.maximum(m_prev, jnp.max(s, axis=1, keepdims=True))
    alpha = jnp.exp(m_prev - m_new)
    p = jnp.exp(s - m_new)
    l_ref[...] = alpha * l_ref[...] + jnp.sum(p, axis=1, keepdims=True)
    if vals_keys_on_lanes:
        pv = lax.dot_general(p.astype(BF16), vals, _NT, preferred_element_type=F32)
    else:
        pv = jnp.dot(p.astype(BF16), vals, preferred_element_type=F32)
    acc_ref[...] = alpha * acc_ref[...] + pv
    m_ref[...] = m_new


def _softmax_init(m_ref, l_ref, acc_ref):
    m_ref[...] = jnp.full(m_ref.shape, NEG_BIG, F32)
    l_ref[...] = jnp.zeros(l_ref.shape, F32)
    acc_ref[...] = jnp.zeros(acc_ref.shape, F32)


def _mla_prompt_kernel(qcat_ref, kcat_ref, kcatt_ref, wuv_ref, o_ref, q_ref, s0_ref, s1_ref, m_ref, l_ref, acc_ref,
                       *, tq, ck):
    qb = pl.program_id(0)
    n_chunks = ((qb + 1) * tq + ck - 1) // ck
    row_pos = qb * tq + lax.broadcasted_iota(I32, (tq, 1), 0)
    for hh in range(MLA_HEADS):
        q_ref[hh * tq:(hh + 1) * tq, :] = qcat_ref[:, hh * QCAT_W:(hh + 1) * QCAT_W]
    _softmax_init(m_ref, l_ref, acc_ref)
    s_bufs = (s0_ref, s1_ref)

    def scores(c, slot):
        off = pl.multiple_of(c * ck, ck)
        s_bufs[slot][...] = jnp.dot(q_ref[...], kcatt_ref[:, pl.ds(off, ck)], preferred_element_type=F32)

    def consume(c, slot):
        off = pl.multiple_of(c * ck, ck)
        col = off + lax.broadcasted_iota(I32, (tq, ck), 1)
        bias = jnp.where(col <= row_pos, 0.0, NEG_BIG)
        s = s_bufs[slot][...] + jnp.concatenate([bias] * MLA_HEADS, axis=0)
        _softmax_step(s, kcat_ref[pl.ds(off, ck), :MLA_KV_LORA], m_ref, l_ref, acc_ref)

    _pipelined_chunks(n_chunks, scores, consume)
    o_lat = (acc_ref[...] / l_ref[...]).astype(BF16)
    lat = jnp.concatenate([o_lat[hh * tq:(hh + 1) * tq, :] for hh in range(MLA_HEADS)], axis=1)
    o_ref[...] = jnp.dot(lat, wuv_ref[...], preferred_element_type=F32)


def _mla_prompt(qcat, kcat, kcatt, wuv, tq=128, ck=1024):
    s_len = qcat.shape[0]
    ck = min(ck, s_len)
    rows = MLA_HEADS * tq
    return pl.pallas_call(
        functools.partial(_mla_prompt_kernel, tq=tq, ck=ck),
        grid=(s_len // tq,),
        in_specs=[pl.BlockSpec((tq, MLA_HEADS * QCAT_W), lambda i: (i, 0)),
                  _resident(kcat.shape), _resident(kcatt.shape), _resident(wuv.shape)],
        out_specs=pl.BlockSpec((tq, MLA_WIDTH), lambda i: (i, 0)),
        out_shape=jax.ShapeDtypeStruct((s_len, MLA_WIDTH), F32),
        scratch_shapes=[pltpu.VMEM((rows, QCAT_W), BF16), pltpu.VMEM((rows, ck), F32), pltpu.VMEM((rows, ck), F32),
                        pltpu.VMEM((rows, 1), F32), pltpu.VMEM((rows, 1), F32), pltpu.VMEM((rows, MLA_KV_LORA), F32)],
        compiler_params=_cparams(("arbitrary",)),
        name="mla_prompt",
    )(qcat, kcat, kcatt, wuv)


PAGES_PER_STEP = 32


def _pad_rows(x):
    return jnp.concatenate([x, jnp.zeros((PAGE_SIZE - x.shape[0], x.shape[1]), x.dtype)], axis=0)


def _tile_rows(x, n):
    return jnp.concatenate([x] * n, axis=0)


def _idx_scores(iq_rows, iw, keys_t16, t):
    r = jnp.dot(iq_rows, keys_t16, preferred_element_type=F32)
    sc = jnp.zeros((t, keys_t16.shape[1]), F32)
    for h in range(IDX_HEADS):
        sc = sc + iw[:, h:h + 1] * jnp.maximum(r[h * t:(h + 1) * t, :], 0.0)
    return sc


def _sample_idx_mla_kernel(pt_ref, iq_ref, iw_ref, ikn_ref, qcat_ref, ckvn_ref, krn_ref, wuv_ref, *rest,
                           pg, n_pages, n_sel, idx_bits, t):
    idx_refs, ckv_refs, kr_refs = rest[:pg], rest[pg:2 * pg], rest[2 * pg:3 * pg]
    o_ref, bias_ref = rest[3 * pg:3 * pg + 2]
    key_ref, iqr_ref, qrow_ref, m_ref, l_ref, acc_ref = rest[3 * pg + 2:]
    j = pl.program_id(1)
    past = n_pages * PAGE_SIZE
    span = pg * PAGE_SIZE

    @pl.when(j == 0)
    def _init():
        iq = iq_ref[...]
        for h in range(IDX_HEADS):
            iqr_ref[h * t:(h + 1) * t, :] = iq[:, h * IDX_DIM:(h + 1) * IDX_DIM]
        qc = qcat_ref[...]
        for hh in range(MLA_HEADS):
            qrow_ref[hh * t:(hh + 1) * t, :] = qc[:, hh * QCAT_W:(hh + 1) * QCAT_W]
        _softmax_init(m_ref, l_ref, acc_ref)

    iqr = iqr_ref[...].astype(BF16)
    iw = iw_ref[...]
    qrow = qrow_ref[...]
    q_lat = qrow[:, :MLA_KV_LORA].astype(BF16)
    q_rope = qrow[:, MLA_KV_LORA + IDX_DIM:MLA_KV_LORA + IDX_DIM + MLA_ROPE].astype(BF16)

    def mla_scores(ck16, kr_t16):
        return (lax.dot_general(q_lat, ck16, _NT, preferred_element_type=F32)
                + jnp.dot(q_rope, kr_t16, preferred_element_type=F32))

    ikc = jnp.concatenate([r[...] for r in idx_refs], axis=1).astype(BF16)
    ckc = jnp.concatenate([r[...] for r in ckv_refs], axis=0).astype(BF16)
    krc = jnp.concatenate([r[...] for r in kr_refs], axis=1).astype(BF16)
    key_ref[:, pl.ds(pl.multiple_of(j * span, span), span)] = _sortable(_idx_scores(iqr, iw, ikc, t))
    _softmax_step(mla_scores(ckc, krc), ckc, m_ref, l_ref, acc_ref)

    @pl.when(j == n_pages // pg - 1)
    def _finish():
        tok = lax.broadcasted_iota(I32, (t, PAGE_SIZE), 0)
        col = lax.broadcasted_iota(I32, (t, PAGE_SIZE), 1)
        vis = col <= tok
        ikn = _pad_rows(ikn_ref[...]).T.astype(BF16)
        key_ref[:, past:past + PAGE_SIZE] = jnp.where(vis, _sortable(_idx_scores(iqr, iw, ikn, t)), INT_MIN)
        ckn = _pad_rows(ckvn_ref[...]).astype(BF16)
        krn = _pad_rows(krn_ref[...]).T.astype(BF16)
        s = mla_scores(ckn, krn) + _tile_rows(jnp.where(vis, 0.0, NEG_BIG), MLA_HEADS)
        _softmax_step(s, ckn, m_ref, l_ref, acc_ref)
        o_lat = acc_ref[...] / l_ref[...]
        lat = jnp.concatenate([o_lat[hh * t:(hh + 1) * t, :] for hh in range(MLA_HEADS)], axis=1)
        o_ref[...] = jnp.dot(lat.astype(BF16), wuv_ref[...], preferred_element_type=F32)

        keys = key_ref[...]
        cols = lax.broadcasted_iota(I32, keys.shape, 1)

        def count(pred):
            return jnp.sum(jnp.where(pred(keys, cols), 1.0, 0.0), axis=1, keepdims=True)

        thr, jj = _select_threshold(count, n_sel, t, idx_bits)
        sel = (keys > thr) | ((keys == thr) & (cols < jj))
        bias_ref[...] = jnp.where(sel, 0.0, NEG_BIG)


def _page_specs(page_shape, pg, n_pages, page_off):
    return [pl.BlockSpec((None,) + page_shape,
                         functools.partial(lambda s, j, pt, i: (page_off + pt[s * n_pages + j * pg + i], 0, 0), i=i))
            for i in range(pg)]


def _key_major(cache, depth_pool):
    return jnp.moveaxis(cache, 2, -1).reshape(depth_pool, -1, PAGE_SIZE)


def _seq_spec(shape):
    return pl.BlockSpec((None,) + tuple(shape[1:]), lambda s, j, pt: (s, 0, 0))


def _sample_idx_mla(pt_flat, iq, iw, ikn, qcat, ckvn, krn, wuv, c_idx, c_ckv, c_kr, n_pages, page_off=0):
    nseq, t, _ = iq.shape
    pg = min(PAGES_PER_STEP, n_pages)
    past = n_pages * PAGE_SIZE
    lp = past + PAGE_SIZE
    n_sel = min(TOPK_MAX, (past + t) // 4)
    rows = MLA_HEADS * t
    kern = functools.partial(_sample_idx_mla_kernel, pg=pg, n_pages=n_pages, n_sel=n_sel,
                             idx_bits=int(past + t).bit_length(), t=t)
    grid_spec = pltpu.PrefetchScalarGridSpec(
        num_scalar_prefetch=1,
        grid=(nseq, n_pages // pg),
        in_specs=[_seq_spec(iq.shape), _seq_spec(iw.shape), _seq_spec(ikn.shape), _seq_spec(qcat.shape),
                  _seq_spec(ckvn.shape), _seq_spec(krn.shape),
                  pl.BlockSpec(wuv.shape, lambda s, j, pt: (0, 0))]
                 + _page_specs((IDX_DIM, PAGE_SIZE), pg, n_pages, page_off)
                 + _page_specs((PAGE_SIZE, MLA_KV_LORA), pg, n_pages, page_off)
                 + _page_specs((MLA_ROPE, PAGE_SIZE), pg, n_pages, page_off),
        out_specs=[pl.BlockSpec((None, t, MLA_WIDTH), lambda s, j, pt: (s, 0, 0)),
                   pl.BlockSpec((None, t, lp), lambda s, j, pt: (s, 0, 0))],
        scratch_shapes=[pltpu.VMEM((t, lp), I32), pltpu.VMEM((IDX_HEADS * t, IDX_DIM), F32),
                        pltpu.VMEM((rows, QCAT_W), F32), pltpu.VMEM((rows, 1), F32),
                        pltpu.VMEM((rows, 1), F32), pltpu.VMEM((rows, MLA_KV_LORA), F32)],
    )
    return pl.pallas_call(
        kern, grid_spec=grid_spec,
        out_shape=[jax.ShapeDtypeStruct((nseq, t, MLA_WIDTH), F32), jax.ShapeDtypeStruct((nseq, t, lp), F32)],
        compiler_params=_cparams(("arbitrary", "arbitrary")),
        name="sample_idx_mla",
    )(pt_flat, iq, iw, ikn, qcat, ckvn, krn, wuv, *([c_idx] * pg), *([c_ckv] * pg), *([c_kr] * pg))


def _sample_dsa_kernel(pt_ref, q_ref, kn_ref, vn_ref, bias_ref, biasn_ref, *rest, pg, n_pages, t):
    k_refs, v_refs = rest[:pg], rest[pg:2 * pg]
    o_ref = rest[2 * pg]
    qbd_ref, m_ref, l_ref, acc_ref = rest[2 * pg + 1:]
    j = pl.program_id(1)
    lane_g = lax.broadcasted_iota(I32, (t, KV_W), 1) // HEAD_DIM

    @pl.when(j == 0)
    def _init():
        q = q_ref[...]
        for r in range(DSA_GROUP):
            qr = q[:, r * KV_W:(r + 1) * KV_W]
            for g in range(DSA_KV_HEADS):
                hh = r * DSA_KV_HEADS + g
                qbd_ref[hh * t:(hh + 1) * t, :] = jnp.where(lane_g == g, qr, 0.0)
        _softmax_init(m_ref, l_ref, acc_ref)

    qbd = qbd_ref[...].astype(BF16)
    kc = jnp.concatenate([r[...] for r in k_refs], axis=1).astype(BF16)
    vc = jnp.concatenate([r[...] for r in v_refs], axis=1).astype(BF16)
    s = jnp.dot(qbd, kc, preferred_element_type=F32) + _tile_rows(bias_ref[...], DSA_HEADS)
    _softmax_step(s, vc, m_ref, l_ref, acc_ref, vals_keys_on_lanes=True)

    @pl.when(j == n_pages // pg - 1)
    def _finish():
        kn = _pad_rows(kn_ref[...]).astype(BF16)
        vn = _pad_rows(vn_ref[...]).astype(BF16)
        s = lax.dot_general(qbd, kn, _NT, preferred_element_type=F32) + _tile_rows(biasn_ref[...], DSA_HEADS)
        _softmax_step(s, vn, m_ref, l_ref, acc_ref)
        o_full = acc_ref[...] / l_ref[...]
        for r in range(DSA_GROUP):
            o_r = jnp.zeros((t, KV_W), F32)
            for g in range(DSA_KV_HEADS):
                hh = r * DSA_KV_HEADS + g
                o_r = o_r + jnp.where(lane_g == g, o_full[hh * t:(hh + 1) * t, :], 0.0)
            o_ref[:, r * KV_W:(r + 1) * KV_W] = o_r


def _sample_dsa(pt_flat, q, kn, vn, bias, c_k, c_v, n_pages, page_off=0):
    nseq, t, _ = q.shape
    pg = min(PAGES_PER_STEP, n_pages)
    rows = DSA_HEADS * t
    span = pg * PAGE_SIZE
    grid_spec = pltpu.PrefetchScalarGridSpec(
        num_scalar_prefetch=1,
        grid=(nseq, n_pages // pg),
        in_specs=[_seq_spec(q.shape), _seq_spec(kn.shape), _seq_spec(vn.shape),
                  pl.BlockSpec((None, t, span), lambda s, j, pt: (s, 0, j)),
                  pl.BlockSpec((None, t, PAGE_SIZE), lambda s, j, pt: (s, 0, n_pages))]
                 + _page_specs((KV_W, PAGE_SIZE), pg, n_pages, page_off)
                 + _page_specs((KV_W, PAGE_SIZE), pg, n_pages, page_off),
        out_specs=pl.BlockSpec((None, t, DSA_WIDTH), lambda s, j, pt: (s, 0, 0)),
        scratch_shapes=[pltpu.VMEM((rows, KV_W), F32), pltpu.VMEM((rows, 1), F32),
                        pltpu.VMEM((rows, 1), F32), pltpu.VMEM((rows, KV_W), F32)],
    )
    return pl.pallas_call(
        functools.partial(_sample_dsa_kernel, pg=pg, n_pages=n_pages, t=t), grid_spec=grid_spec,
        out_shape=jax.ShapeDtypeStruct((nseq, t, DSA_WIDTH), F32),
        compiler_params=_cparams(("arbitrary", "arbitrary")),
        name="sample_dsa",
    )(pt_flat, q, kn, vn, bias, bias, *([c_k] * pg), *([c_v] * pg))


def _merge_kernel(x_ref, od_ref, om_ref, gate_ref, gd_ref, gm_ref, wo_ref, o_ref):
    a = jnp.concatenate([_rms(od_ref[...]) * gd_ref[...], _rms(om_ref[...]) * gm_ref[...]], axis=1)
    y = jnp.dot(a.astype(BF16), wo_ref[...], preferred_element_type=F32)
    o_ref[...] = x_ref[...] + gate_ref[...] * y


def _mod_spec(a, tm):
    if a.shape[0] == 1:
        return pl.BlockSpec((1, a.shape[1]), lambda i, *_: (0, 0))
    return pl.BlockSpec((tm, a.shape[1]), lambda i, *_: (i, 0))


def _merge(x, o_dsa, o_mla, gate, g_dsa, g_mla, w_out16, tm=256):
    t = x.shape[0]
    tm = min(tm, t)

    def row(w):
        return pl.BlockSpec((tm, w), lambda i: (i, 0))

    return pl.pallas_call(
        _merge_kernel,
        grid=(t // tm,),
        in_specs=[row(D_MODEL), row(DSA_WIDTH), row(MLA_WIDTH), _mod_spec(gate, tm),
                  _full_spec((1, DSA_WIDTH)), _full_spec((1, MLA_WIDTH)), _full_spec(w_out16.shape)],
        out_specs=row(D_MODEL),
        out_shape=jax.ShapeDtypeStruct((t, D_MODEL), F32),
        compiler_params=_cparams(("arbitrary",)),
        name="merge",
    )(x, o_dsa, o_mla, gate, g_dsa.reshape(1, -1), g_mla.reshape(1, -1), w_out16)


EXPERT_CHUNK = 2048
I_PER_CHUNK = EXPERT_CHUNK // PEER_NKEYS


def _erf(x):
    return lax.erf(x)


def _extract_top(cur, n):
    vals = []
    for _ in range(n):
        mx = jnp.max(cur, axis=0, keepdims=True)
        vals.append(mx)
        cur = jnp.where(cur == mx, -jnp.inf, cur)
    return vals


def _peer_kernel(x1_ref, shift_ref, scale_ref, gate_ref, gffn_ref, gfin_ref, wpq_ref, keys_ref, u_ref, vt_ref,
                 y_ref, ht_ref, qry_ref, s0_ref, s1_ref, e0_ref, e1_ref, tau_ref, act_ref, ga_ref, acc_ref,
                 *, tm, final_norm):
    c = pl.program_id(1)

    @pl.when(c == 0)
    def _prep():
        h = _rms(x1_ref[...]) * gffn_ref[...] * (1.0 + scale_ref[...]) + shift_ref[...]
        ht_ref[...] = h.T.astype(BF16)
        qry_ref[...] = jnp.dot(wpq_ref[...], ht_ref[...], preferred_element_type=F32)
        acc_ref[...] = jnp.zeros(acc_ref.shape, F32)

        def head(hh, carry):
            tops = []
            for p, s_ref in ((0, s0_ref), (1, s1_ref)):
                blk = pl.multiple_of((hh * 2 + p) * PEER_HALF, PEER_HALF)
                sub = jnp.dot(keys_ref[pl.ds(blk, PEER_NKEYS), :], qry_ref[pl.ds(blk, PEER_HALF), :],
                              preferred_element_type=F32, precision=lax.Precision.HIGHEST)
                s_ref[hh] = sub
                tops.append(_extract_top(sub, PEER_TOPK))
            top0, top1 = tops
            t1 = jnp.concatenate(top1, axis=0)
            cand = jnp.concatenate([top0[a] + t1 for a in range(PEER_TOPK)], axis=0)
            tau = _extract_top(cand, PEER_TOPK)[-1]
            cmax = top0[0] + top1[0]
            z = jnp.sum(jnp.where(cand >= tau, jnp.exp(cand - cmax), 0.0), axis=0, keepdims=True)
            e0_ref[hh] = jnp.exp(s0_ref[hh] - top0[0]) / z
            e1_ref[hh] = jnp.exp(s1_ref[hh] - top1[0])
            tau_ref[pl.ds(hh, 1), :] = tau
            return carry

        lax.fori_loop(0, PEER_HEADS, head, 0)

    a = jnp.dot(u_ref[...], ht_ref[...], preferred_element_type=F32)
    act_ref[...] = 0.5 * a * (1.0 + _erf(a * (2.0 ** -0.5)))

    def i_slab(il, carry):
        i = c * I_PER_CHUNK + il
        g = jnp.zeros((PEER_NKEYS, tm), F32)
        for hh in range(PEER_HEADS):
            tsum = s0_ref[hh, pl.ds(i, 1), :] + s1_ref[hh]
            hit = tsum >= tau_ref[hh:hh + 1, :]
            g = g + jnp.where(hit, e1_ref[hh], 0.0) * e0_ref[hh, pl.ds(i, 1), :]
        rows = pl.ds(pl.multiple_of(il * PEER_NKEYS, PEER_NKEYS), PEER_NKEYS)
        ga_ref[rows, :] = (g * act_ref[rows, :]).astype(BF16)
        return carry

    lax.fori_loop(0, I_PER_CHUNK, i_slab, 0)
    acc_ref[...] += jnp.dot(vt_ref[...], ga_ref[...], preferred_element_type=F32)

    @pl.when(c == pl.num_programs(1) - 1)
    def _finish():
        x2 = x1_ref[...] + gate_ref[...] * acc_ref[...].T
        y_ref[...] = _rms(x2) * gfin_ref[...] if final_norm else x2


def _peer_final(x1, shift, scale, gate, g_ffn, g_final, wpq_t16, keys2d, u16, vt16, final_norm, tm=256):
    t = x1.shape[0]
    tm = min(tm, t)
    n_chunks = N_EXPERTS // EXPERT_CHUNK
    hshape = (PEER_HEADS, PEER_NKEYS, tm)
    return pl.pallas_call(
        functools.partial(_peer_kernel, tm=tm, final_norm=final_norm),
        grid=(t // tm, n_chunks),
        in_specs=[pl.BlockSpec((tm, D_MODEL), lambda i, c: (i, 0)),
                  _mod_spec(shift, tm), _mod_spec(scale, tm), _mod_spec(gate, tm),
                  _full_spec((1, D_MODEL)), _full_spec((1, D_MODEL)),
                  _resident(wpq_t16.shape), _resident(keys2d.shape),
                  pl.BlockSpec((EXPERT_CHUNK, D_MODEL), lambda i, c: (c, 0)),
                  pl.BlockSpec((D_MODEL, EXPERT_CHUNK), lambda i, c: (0, c))],
        out_specs=pl.BlockSpec((tm, D_MODEL), lambda i, c: (i, 0)),
        out_shape=jax.ShapeDtypeStruct((t, D_MODEL), F32),
        scratch_shapes=[pltpu.VMEM((D_MODEL, tm), BF16), pltpu.VMEM((2 * PEER_HEADS * PEER_HALF, tm), F32),
                        pltpu.VMEM(hshape, F32), pltpu.VMEM(hshape, F32), pltpu.VMEM(hshape, F32),
                        pltpu.VMEM(hshape, F32), pltpu.VMEM((PEER_HEADS, tm), F32),
                        pltpu.VMEM((EXPERT_CHUNK, tm), F32), pltpu.VMEM((EXPERT_CHUNK, tm), BF16),
                        pltpu.VMEM((D_MODEL, tm), F32)],
        compiler_params=_cparams(("arbitrary", "arbitrary")),
        name="peer_final",
    )(x1, shift, scale, gate, g_ffn.reshape(1, -1), g_final.reshape(1, -1), wpq_t16, keys2d, u16, vt16)


def _mod_rows(m, k, n_rep):
    rows = m[:, k * D_MODEL:(k + 1) * D_MODEL]
    return rows if rows.shape[0] == 1 else jnp.repeat(rows, n_rep, axis=0)


def _perm_heads(a):
    tail = a.shape[1:]
    return a.reshape((DSA_KV_HEADS, DSA_GROUP, HEAD_DIM) + tail).swapaxes(0, 1).reshape((DSA_WIDTH,) + tail)


def kernel(x_prompt, x_sample, cache_dsa_k, cache_dsa_v, cache_idx_k, cache_mla_ckv, cache_mla_krope, page_table, c_prompt, c_sample, w_mod, b_mod, g_mix, g_ffn, w_in, g_cq, g_ckv, w_uq, w_uk, w_uv, g_out_dsa, g_out_mla, w_out, w_pq, peer_keys, peer_u, peer_v, g_final):
    depth = w_mod.shape[0]
    b, s_len, _ = x_prompt.shape
    nseq, t, _ = x_sample.shape
    n_pages = page_table.shape[1]
    past = n_pages * PAGE_SIZE
    n_pool = cache_dsa_k.shape[1]
    xp = x_prompt.reshape(b * s_len, D_MODEL)
    xs = x_sample.reshape(nseq * t, D_MODEL)
    cos_p, sin_p = _rope_tables(jnp.tile(jnp.arange(s_len), b))
    cos_s, sin_s = _rope_tables(jnp.tile(past + jnp.arange(t), nseq))
    pt_flat = page_table.reshape(-1)
    c_k = _key_major(cache_dsa_k, depth * n_pool)
    c_v = _key_major(cache_dsa_v, depth * n_pool)
    c_idx = _key_major(cache_idx_k, depth * n_pool)
    c_kr = _key_major(cache_mla_krope, depth * n_pool)
    c_ckv = cache_mla_ckv.reshape(depth * n_pool, PAGE_SIZE, MLA_KV_LORA)
    new_p = [[] for _ in range(5)]
    new_s = [[] for _ in range(5)]

    for l in range(depth):
        last = l == depth - 1
        mod = _adaln(jnp.concatenate([c_prompt, c_sample], axis=0), w_mod[l], b_mod[l])
        mp, ms = mod[:b], mod[b:]
        wqn, wqr, wqrr, wuk, wuv = _pack_mla_weights(w_uq[l], w_uk[l], w_uv[l])
        packed = _pack_in_weights(w_in[l]) + (wqn, wqr, wqrr, wuk)
        g_dsa = _perm_heads(g_out_dsa[l])
        w_out16 = jnp.concatenate([_perm_heads(w_out[l][:DSA_WIDTH]), w_out[l][DSA_WIDTH:]], axis=0).astype(BF16)
        peer_w = (w_pq[l].T.astype(BF16), peer_keys[l].reshape(2 * PEER_HEADS * PEER_NKEYS, PEER_HALF),
                  peer_u[l].astype(BF16), peer_v[l].T.astype(BF16))

        (k32, v32, ik32, ckv32, kr32, q16, iq16, iw, v16, kcat, qcat, kt16, ikt16, kcatt) = _project(
            xp, _mod_rows(mp, 0, s_len), _mod_rows(mp, 1, s_len), g_mix[l], packed, cos_p, sin_p, g_cq[l], g_ckv[l])
        o_dsa, o_mla = [], []
        for bi in range(b):
            sl = slice(bi * s_len, (bi + 1) * s_len)
            o_dsa.append(_dsa_prompt(q16[sl], iq16[sl], iw[sl], ikt16[:, sl], kt16[:, sl], v16[sl]))
            o_mla.append(_mla_prompt(qcat[sl], kcat[sl], kcatt[:, sl], wuv))
        o_dsa = o_dsa[0] if b == 1 else jnp.concatenate(o_dsa, axis=0)
        o_mla = o_mla[0] if b == 1 else jnp.concatenate(o_mla, axis=0)
        x1 = _merge(xp, o_dsa, o_mla, _mod_rows(mp, 2, s_len), g_dsa, g_out_mla[l], w_out16)
        xp = _peer_final(x1, _mod_rows(mp, 3, s_len), _mod_rows(mp, 4, s_len), _mod_rows(mp, 5, s_len),
                         g_ffn[l], g_final, *peer_w, final_norm=last)
        for lst, (val, shp) in zip(new_p, ((k32, (DSA_KV_HEADS, HEAD_DIM)), (v32, (DSA_KV_HEADS, HEAD_DIM)),
                                           (ik32, (IDX_DIM,)), (ckv32, (MLA_KV_LORA,)), (kr32, (MLA_ROPE,)))):
            lst.append(val.reshape((b, s_len) + shp))

        ms_tok = [jnp.repeat(ms[:, k * D_MODEL:(k + 1) * D_MODEL], t, axis=0) for k in range(6)]
        (k32, v32, ik32, ckv32, kr32, q16, iq16, iw, v16, kcat, qcat, kt16, ikt16, kcatt) = _project(
            xs, ms_tok[0], ms_tok[1], g_mix[l], packed, cos_s, sin_s, g_cq[l], g_ckv[l], tm=min(256, nseq * t))

        def seq3(a):
            return a.reshape(nseq, t, a.shape[-1]).astype(F32)

        o_mla, bias = _sample_idx_mla(pt_flat, seq3(iq16), seq3(iw), seq3(ik32), seq3(qcat), seq3(ckv32),
                                      seq3(kr32), wuv, c_idx, c_ckv, c_kr, n_pages, page_off=l * n_pool)
        o_dsa = _sample_dsa(pt_flat, seq3(q16), seq3(k32), seq3(v32), bias, c_k, c_v, n_pages,
                            page_off=l * n_pool)
        x1 = _merge(xs, o_dsa.reshape(nseq * t, DSA_WIDTH), o_mla.reshape(nseq * t, MLA_WIDTH), ms_tok[2],
                    g_dsa, g_out_mla[l], w_out16)
        xs = _peer_final(x1, ms_tok[3], ms_tok[4], ms_tok[5], g_ffn[l], g_final, *peer_w, final_norm=last)
        for lst, (val, shp) in zip(new_s, ((k32, (DSA_KV_HEADS, HEAD_DIM)), (v32, (DSA_KV_HEADS, HEAD_DIM)),
                                           (ik32, (IDX_DIM,)), (ckv32, (MLA_KV_LORA,)), (kr32, (MLA_ROPE,)))):
            lst.append(val.reshape((nseq, t) + shp))

    sp = [jnp.stack(a, axis=0) for a in new_p]
    ss = [jnp.stack(a, axis=0) for a in new_s]
    return (xp.reshape(b, s_len, D_MODEL), xs.reshape(nseq, t, D_MODEL),
            sp[0], sp[1], sp[2], sp[3], sp[4], ss[0], ss[1], ss[2], ss[3], ss[4])
```

```python
import functools

import numpy as np
import jax
import jax.numpy as jnp
from jax import lax
from jax.experimental import pallas as pl
from jax.experimental.pallas import tpu as pltpu

F32 = jnp.float32
BF16 = jnp.bfloat16
I32 = jnp.int32

D_MODEL = 1024
HEAD_DIM = 64
DSA_WIDTH = 512
DSA_HEADS = 8
DSA_KV_HEADS = 4
DSA_GROUP = 2
KV_W = DSA_KV_HEADS * HEAD_DIM
IDX_HEADS = 4
IDX_DIM = 64
TOPK_MAX = 256
MLA_V = 64
MLA_WIDTH = 512
MLA_HEADS = 8
MLA_NOPE = 64
MLA_ROPE = 32
MLA_Q_LORA = 384
MLA_KV_LORA = 256
MLA_SCALE = (MLA_NOPE + MLA_ROPE) ** -0.5
PEER_HEADS = 8
PEER_NKEYS = 128
N_EXPERTS = PEER_NKEYS * PEER_NKEYS
PEER_HALF = 128
PEER_TOPK = 16
PAGE_SIZE = 128
ROPE_THETA = 10000.0
EPS = 1e-6
IN_SPLITS = (DSA_WIDTH, KV_W, KV_W, IDX_HEADS * IDX_DIM, IDX_HEADS, IDX_DIM,
             MLA_Q_LORA, MLA_KV_LORA, MLA_ROPE)

LANES = 128
SUBLANES = 8
VMEM_LIMIT = 56 * 1024 * 1024

C_Q, C_K, C_IQ, C_D, C_V, C_CKV, C_CQ, C_IW, C_END = 0, 512, 768, 1024, 1152, 1408, 1664, 2048, 2176
ROT_W = C_V
QCAT_W = 384
INT_MIN = -2147483648
INT_MAX = 2147483647
NEG_BIG = -1e30


def _cparams(sem):
    return pltpu.CompilerParams(dimension_semantics=sem, vmem_limit_bytes=VMEM_LIMIT)


def _full_spec(shape):
    nd = len(shape)
    return pl.BlockSpec(shape, lambda *_: (0,) * nd)


def _adaln_kernel(c_ref, w_ref, b_ref, o_ref):
    c = c_ref[...]
    a = c / (1.0 + jnp.exp(-c))
    o_ref[...] = jnp.dot(a, w_ref[...], preferred_element_type=F32,
                         precision=lax.Precision.HIGHEST) + b_ref[...]


def _adaln(c, w_mod, b_mod):
    n = c.shape[0]
    n_pad = -(-n // SUBLANES) * SUBLANES
    cp = jnp.pad(c, ((0, n_pad - n), (0, 0)))
    tn = 1536
    out = pl.pallas_call(
        _adaln_kernel,
        grid=(w_mod.shape[1] // tn,),
        in_specs=[pl.BlockSpec((n_pad, D_MODEL), lambda j: (0, 0)),
                  pl.BlockSpec((D_MODEL, tn), lambda j: (0, j)),
                  pl.BlockSpec((1, tn), lambda j: (0, j))],
        out_specs=pl.BlockSpec((n_pad, tn), lambda j: (0, j)),
        out_shape=jax.ShapeDtypeStruct((n_pad, w_mod.shape[1]), F32),
        compiler_params=_cparams(("arbitrary",)),
        name="adaln",
    )(cp, w_mod, b_mod.reshape(1, -1))
    return out[:n]


def _swap_halves(w, hd):
    d, n = w.shape
    w4 = w.reshape(d, n // hd, 2, hd // 2)
    return w4[:, :, ::-1, :].reshape(d, n)


def _pack_in_weights(w_in):
    cuts = [int(v) for v in np.cumsum(IN_SPLITS)[:-1]]
    wq, wk, wv, wiq, wiw, wik, wcq, wckv, wkr = jnp.split(w_in, cuts, axis=1)
    wq = wq.reshape(D_MODEL, DSA_KV_HEADS, DSA_GROUP, HEAD_DIM).transpose(0, 2, 1, 3).reshape(D_MODEL, DSA_WIDTH)
    pad32 = jnp.zeros((D_MODEL, 32), F32)
    w_d = jnp.concatenate([wik, wkr, pad32], axis=1)
    w_d_rot = jnp.concatenate([_swap_halves(wik, IDX_DIM), _swap_halves(wkr, MLA_ROPE), pad32], axis=1)
    wiw_p = jnp.pad(wiw, ((0, 0), (0, LANES - IDX_HEADS)))
    w_main = jnp.concatenate([wq, wk, wiq, w_d, wv, wckv, wcq, wiw_p], axis=1)
    w_rot = jnp.concatenate([_swap_halves(wq, HEAD_DIM), _swap_halves(wk, HEAD_DIM),
                             _swap_halves(wiq, IDX_DIM), w_d_rot], axis=1)
    return w_main.astype(BF16), w_rot.astype(BF16)


def _pack_mla_weights(w_uq, w_uk, w_uv):
    w_n = w_uq[:, :, :MLA_NOPE].reshape(MLA_Q_LORA, MLA_HEADS * MLA_NOPE)
    w_r = w_uq[:, :, MLA_NOPE:]
    w_r_rot = jnp.concatenate([w_r[..., MLA_ROPE // 2:], w_r[..., :MLA_ROPE // 2]], axis=-1)

    def place(w):
        z = jnp.zeros((MLA_Q_LORA, MLA_HEADS, LANES), F32)
        return z.at[:, :, IDX_DIM:IDX_DIM + MLA_ROPE].set(w).reshape(MLA_Q_LORA, MLA_HEADS * LANES)

    eye = jnp.eye(MLA_HEADS, dtype=F32)
    uk = jnp.einsum('chd,hk->hdkc', w_uk, eye).reshape(MLA_HEADS * MLA_NOPE, MLA_HEADS * MLA_KV_LORA)
    uv = jnp.einsum('chd,hk->hckd', w_uv, eye).reshape(MLA_HEADS * MLA_KV_LORA, MLA_HEADS * MLA_V)
    return (w_n.astype(BF16), place(w_r).astype(BF16), place(w_r_rot).astype(BF16),
            uk.astype(BF16), uv.astype(BF16))


def _rope_tables(pos):
    pos = pos.astype(F32)[:, None]

    def cs(dim):
        half = dim // 2
        inv = ROPE_THETA ** (-jnp.arange(half, dtype=F32) / half)
        ang = pos * inv[None, :]
        c, s = jnp.cos(ang), jnp.sin(ang)
        return jnp.concatenate([c, c], axis=1), jnp.concatenate([-s, s], axis=1)

    c64, s64 = cs(HEAD_DIM)
    c32, s32 = cs(MLA_ROPE)
    z = jnp.zeros((pos.shape[0], 32), F32)
    return (jnp.concatenate([c64, c64, c64, c32, z], axis=1),
            jnp.concatenate([s64, s64, s64, s32, z], axis=1))


def _rms(x):
    return x * lax.rsqrt(jnp.mean(x * x, axis=-1, keepdims=True) + EPS)


def _proj_kernel(x_ref, shift_ref, scale_ref, gmix_ref, wm_ref, wr_ref, cos_ref, sin_ref,
                 gcq_ref, gckv_ref, wqn_ref, wqr_ref, wqrr_ref, wuk_ref,
                 k32_ref, v32_ref, ik32_ref, ckv32_ref, kr32_ref,
                 q16_ref, iq16_ref, iw_ref, v16_ref, kcat_ref, qcat_ref, kt16_ref, ikt16_ref, kcatt_ref):
    x = x_ref[...]
    h = _rms(x) * gmix_ref[...] * (1.0 + scale_ref[...]) + shift_ref[...]
    hb = h.astype(BF16)
    main = jnp.dot(hb, wm_ref[...], preferred_element_type=F32)
    rot = jnp.dot(hb, wr_ref[...], preferred_element_type=F32)
    cos = cos_ref[...]
    sin = sin_ref[...]
    cos_h, sin_h = cos[:, :LANES], sin[:, :LANES]
    cos_d, sin_d = cos[:, LANES:], sin[:, LANES:]

    def rope(lo, hi):
        reps = (hi - lo) // LANES
        c = jnp.concatenate([cos_h] * reps, axis=1)
        s = jnp.concatenate([sin_h] * reps, axis=1)
        return main[:, lo:hi] * c + rot[:, lo:hi] * s

    q = rope(C_Q, C_K)
    k = rope(C_K, C_IQ)
    iq = rope(C_IQ, C_D)
    sec_d = main[:, C_D:C_V] * cos_d + rot[:, C_D:C_V] * sin_d
    v = main[:, C_V:C_CKV]
    ckv = _rms(main[:, C_CKV:C_CQ]) * gckv_ref[...]
    cqn = (_rms(main[:, C_CQ:C_IW]) * gcq_ref[...]).astype(BF16)

    k32_ref[...] = k
    v32_ref[...] = v
    ik32_ref[...] = sec_d[:, :IDX_DIM]
    ckv32_ref[...] = ckv
    kr32_ref[...] = sec_d[:, IDX_DIM:IDX_DIM + MLA_ROPE]
    q16_ref[...] = (q * HEAD_DIM ** -0.5).astype(BF16)
    iq16_ref[...] = iq.astype(BF16)
    iw_ref[...] = main[:, C_IW:C_END] * (IDX_HEADS ** -0.5 * IDX_DIM ** -0.5)
    v16_ref[...] = v.astype(BF16)
    lane = lax.broadcasted_iota(I32, sec_d.shape, 1)
    kr_blk = jnp.where((lane >= IDX_DIM) & (lane < IDX_DIM + MLA_ROPE), sec_d, 0.0)
    kcat_ref[...] = jnp.concatenate([ckv, kr_blk], axis=1).astype(BF16)
    kt16_ref[...] = k.T.astype(BF16)
    ikt16_ref[...] = sec_d.T[:IDX_DIM, :].astype(BF16)
    kcatt_ref[...] = jnp.concatenate([ckv.T, kr_blk.T], axis=0).astype(BF16)

    qn = jnp.dot(cqn, wqn_ref[...], preferred_element_type=F32).astype(BF16)
    qlat = jnp.dot(qn, wuk_ref[...], preferred_element_type=F32) * MLA_SCALE
    qr_m = jnp.dot(cqn, wqr_ref[...], preferred_element_type=F32)
    qr_r = jnp.dot(cqn, wqrr_ref[...], preferred_element_type=F32)
    cd = jnp.concatenate([cos_d] * MLA_HEADS, axis=1)
    sd = jnp.concatenate([sin_d] * MLA_HEADS, axis=1)
    qr = (qr_m * cd + qr_r * sd) * MLA_SCALE
    for hh in range(MLA_HEADS):
        qcat_ref[:, hh * QCAT_W:hh * QCAT_W + MLA_KV_LORA] = (
            qlat[:, hh * MLA_KV_LORA:(hh + 1) * MLA_KV_LORA].astype(BF16))
        qcat_ref[:, hh * QCAT_W + MLA_KV_LORA:(hh + 1) * QCAT_W] = (
            qr[:, hh * LANES:(hh + 1) * LANES].astype(BF16))


def _project(x, shift, scale, g_mix, packed, cos_t, sin_t, g_cq, g_ckv, tm=256):
    t = x.shape[0]
    w_main, w_rot, wqn, wqr, wqrr, wuk = packed
    per_tok = shift.shape[0] != 1
    mod_spec = pl.BlockSpec((tm, D_MODEL), lambda i: (i, 0)) if per_tok else _full_spec((1, D_MODEL))

    def row(w):
        return pl.BlockSpec((tm, w), lambda i: (i, 0))

    widths = [(KV_W, F32), (KV_W, F32), (IDX_DIM, F32), (MLA_KV_LORA, F32), (MLA_ROPE, F32),
              (DSA_WIDTH, BF16), (IDX_HEADS * IDX_DIM, BF16), (LANES, F32), (KV_W, BF16),
              (QCAT_W, BF16), (MLA_HEADS * QCAT_W, BF16)]
    heights = [KV_W, IDX_DIM, QCAT_W]
    return pl.pallas_call(
        _proj_kernel,
        grid=(t // tm,),
        in_specs=[row(D_MODEL), mod_spec, mod_spec, _full_spec((1, D_MODEL)),
                  _full_spec(w_main.shape), _full_spec(w_rot.shape), row(2 * LANES), row(2 * LANES),
                  _full_spec((1, MLA_Q_LORA)), _full_spec((1, MLA_KV_LORA)),
                  _full_spec(wqn.shape), _full_spec(wqr.shape), _full_spec(wqrr.shape), _full_spec(wuk.shape)],
        out_specs=[row(w) for w, _ in widths] + [pl.BlockSpec((h, tm), lambda i: (0, i)) for h in heights],
        out_shape=[jax.ShapeDtypeStruct((t, w), dt) for w, dt in widths]
                  + [jax.ShapeDtypeStruct((h, t), BF16) for h in heights],
        compiler_params=_cparams(("arbitrary",)),
        name="project",
    )(x, shift, scale, g_mix.reshape(1, -1), w_main, w_rot, cos_t, sin_t,
      g_cq.reshape(1, -1), g_ckv.reshape(1, -1), wqn, wqr, wqrr, wuk)


def _sortable(s):
    b = pltpu.bitcast(s + 0.0, I32)
    return b ^ ((b >> 31) & INT_MAX)


_NT = (((1,), (1,)), ((), ()))


def _fold_lanes(x):
    part = x[:, :LANES]
    for j in range(1, x.shape[1] // LANES):
        part = part + x[:, j * LANES:(j + 1) * LANES]
    return part


def _select_threshold(count, n_sel, rows, idx_bits):
    def bit_step(i, t):
        cand = t ^ lax.shift_left(jnp.int32(1), 31 - i)
        return jnp.where(count(lambda k, c: k >= cand) >= n_sel, cand, t)

    t = lax.fori_loop(0, 32, bit_step, jnp.full((rows, 1), INT_MIN, I32))
    need = n_sel - count(lambda k, c: k > t)
    n_eq = count(lambda k, c: k == t)
    return t, _resolve_ties(count, t, need, n_eq, rows, idx_bits)


def _resolve_ties(count, t, need, n_eq, rows, idx_bits):
    live = t != INT_MIN
    tie = jnp.max(jnp.where((n_eq > need) & live, 1.0, 0.0)) > 0.0

    def tie_search():
        def step(i, j):
            cand = j | lax.shift_left(jnp.int32(1), idx_bits - 1 - i)
            return jnp.where(count(lambda k, c: (k == t) & (c < cand)) <= need, cand, j)
        return lax.fori_loop(0, idx_bits, step, jnp.zeros((rows, 1), I32))

    j = lax.cond(tie, tie_search, lambda: jnp.full((rows, 1), INT_MAX, I32))
    return jnp.where(live, j, 0)


_TRANSPOSE_MASKS = ((16, 0x0000FFFF), (8, 0x00FF00FF), (4, 0x0F0F0F0F), (2, 0x33333333), (1, 0x55555555))
WORD_BITS = 32


def _bit_transpose32(a):
    a = list(a)
    for sh, mask in _TRANSPOSE_MASKS:
        for k in range(WORD_BITS):
            if k & sh:
                continue
            t = (a[k] ^ lax.shift_right_logical(a[k + sh], jnp.int32(sh))) & mask
            a[k] = a[k] ^ t
            a[k + sh] = a[k + sh] ^ lax.shift_left(t, jnp.int32(sh))
    return a


def _popcount_rows(x):
    return jnp.sum(_fold_lanes(lax.population_count(x)).astype(F32), axis=1, keepdims=True)


def _pipelined_chunks(n_chunks, scores, consume):
    scores(0, 0)

    def pair(i, carry):
        c = 2 * i
        scores(c + 1, 1)
        consume(c, 0)
        scores(jnp.minimum(c + 2, n_chunks - 1), 0)
        consume(c + 1, 1)
        return carry

    lax.fori_loop(0, n_chunks // 2, pair, 0)

    @pl.when(n_chunks % 2 == 1)
    def _():
        consume(n_chunks - 1, 0)


def _dsa_prompt_kernel(q_ref, iq_ref, iw_ref, ikt_ref, kt_ref, v_ref, o_ref,
                       key_ref, plane_ref, alive_ref, qbd_ref, s0_ref, s1_ref, m_ref, l_ref, acc_ref,
                       *, tq, ck, cka, n_sel, idx_bits, nblk):
    qb = pl.program_id(0)
    n_att = ((qb + 1) * tq + cka - 1) // cka
    n_chunks = n_att * (cka // ck)
    row_pos = qb * tq + lax.broadcasted_iota(I32, (tq, 1), 0)
    iw = iw_ref[...]
    iq = iq_ref[...]
    iq_h = [iq[:, h * IDX_DIM:(h + 1) * IDX_DIM] for h in range(IDX_HEADS)]
    iw_h = [iw[:, h:h + 1] for h in range(IDX_HEADS)]

    def cols(off):
        return off + lax.broadcasted_iota(I32, (tq, ck), 1)

    def score_chunk(c, carry):
        off = pl.multiple_of(c * ck, ck)
        ikc = ikt_ref[:, pl.ds(off, ck)]
        sc = jnp.zeros((tq, ck), F32)
        for h in range(IDX_HEADS):
            r = jnp.dot(iq_h[h], ikc, preferred_element_type=F32)
            sc = sc + iw_h[h] * jnp.maximum(r, 0.0)
        key_ref[:, pl.ds(off, ck)] = jnp.where(cols(off) <= row_pos, _sortable(sc), INT_MIN)
        return carry

    lax.fori_loop(0, n_chunks, score_chunk, 0)

    def count(pred):
        def body(c, cnt):
            off = pl.multiple_of(c * ck, ck)
            hit = pred(key_ref[:, pl.ds(off, ck)], cols(off))
            return cnt + _fold_lanes(jnp.where(hit, 1.0, 0.0))
        cnt = lax.fori_loop(0, n_chunks, body, jnp.zeros((tq, LANES), F32))
        return jnp.sum(cnt, axis=1, keepdims=True)

    group = nblk * LANES
    n_groups = (n_chunks * ck + group - 1) // group

    def fill(c, carry):
        key_ref[:, pl.ds(pl.multiple_of(c * ck, ck), ck)] = jnp.full((tq, ck), INT_MIN, I32)
        return carry

    lax.fori_loop(n_chunks, n_groups * (group // ck), fill, 0)
    pad = [jnp.full((SUBLANES, LANES), INT_MIN, I32)] * (WORD_BITS - nblk)

    def to_planes(idx, carry):
        g = idx // (tq // SUBLANES)
        r0 = pl.multiple_of((idx % (tq // SUBLANES)) * SUBLANES, SUBLANES)
        c0 = pl.multiple_of(g * group, group)
        words = [key_ref[pl.ds(r0, SUBLANES), pl.ds(c0 + jj * LANES, LANES)] for jj in range(nblk)] + pad
        planes = _bit_transpose32(words)
        planes[0] = ~planes[0]
        for b in range(WORD_BITS):
            plane_ref[b, pl.ds(r0, SUBLANES), pl.ds(pl.multiple_of(g * LANES, LANES), LANES)] = planes[b]
        return carry

    lax.fori_loop(0, n_groups * (tq // SUBLANES), to_planes, 0)
    w_all = alive_ref.shape[1]
    in_span = lax.broadcasted_iota(I32, (tq, w_all), 1) < n_groups * LANES
    alive_ref[...] = jnp.where(in_span, -1, 0).astype(I32)

    def bit_step(b, carry):
        k_rem, t_u = carry
        alive = alive_ref[...]
        ones = alive & plane_ref[b]
        cnt = _popcount_rows(ones)
        take = cnt >= k_rem
        alive_ref[...] = jnp.where(take, ones, alive ^ ones)
        return (jnp.where(take, k_rem, k_rem - cnt),
                jnp.where(take, t_u | lax.shift_left(jnp.int32(1), 31 - b), t_u))

    need, t_u = lax.fori_loop(0, WORD_BITS, bit_step,
                              (jnp.full((tq, 1), float(n_sel), F32), jnp.zeros((tq, 1), I32)))
    t = t_u ^ INT_MIN
    j = _resolve_ties(count, t, need, _popcount_rows(alive_ref[...]), tq, idx_bits)

    q = q_ref[...]
    lane_g = lax.broadcasted_iota(I32, (tq, KV_W), 1) // HEAD_DIM
    for r in range(DSA_GROUP):
        qr = q[:, r * KV_W:(r + 1) * KV_W]
        for g in range(DSA_KV_HEADS):
            hh = r * DSA_KV_HEADS + g
            qbd_ref[hh * tq:(hh + 1) * tq, :] = jnp.where(lane_g == g, qr, jnp.zeros_like(qr))
    _softmax_init(m_ref, l_ref, acc_ref)

    s_bufs = (s0_ref, s1_ref)

    def scores(c, slot):
        off = pl.multiple_of(c * cka, cka)
        s_bufs[slot][...] = jnp.dot(qbd_ref[...], kt_ref[:, pl.ds(off, cka)], preferred_element_type=F32)

    def consume(c, slot):
        off = pl.multiple_of(c * cka, cka)
        kc = key_ref[:, pl.ds(off, cka)]
        col = off + lax.broadcasted_iota(I32, (tq, cka), 1)
        sel = (kc > t) | ((kc == t) & (col < j))
        bias = jnp.where(sel, 0.0, NEG_BIG)
        s = s_bufs[slot][...] + jnp.concatenate([bias] * DSA_HEADS, axis=0)
        _softmax_step(s, v_ref[pl.ds(off, cka), :], m_ref, l_ref, acc_ref)

    _pipelined_chunks(n_att, scores, consume)
    o_full = acc_ref[...] / l_ref[...]
    for r in range(DSA_GROUP):
        o_r = jnp.zeros((tq, KV_W), F32)
        for g in range(DSA_KV_HEADS):
            hh = r * DSA_KV_HEADS + g
            o_r = o_r + jnp.where(lane_g == g, o_full[hh * tq:(hh + 1) * tq, :], 0.0)
        o_ref[:, r * KV_W:(r + 1) * KV_W] = o_r


def _resident(shape):
    nd = len(shape)
    return pl.BlockSpec(shape, lambda *_: (0,) * nd, pipeline_mode=pl.Buffered(1))


def _dsa_prompt(q16, iq16, iw, ikt16, kt16, v16, tq=128, ck=512, cka=1024):
    s_len = q16.shape[0]
    cka = min(cka, s_len)
    ck = min(ck, cka)
    n_sel = min(TOPK_MAX, s_len // 4)
    rows = DSA_HEADS * tq
    nblk = min(WORD_BITS, s_len // LANES)
    assert s_len % (nblk * LANES) == 0 and (nblk * LANES) % ck == 0
    n_words = s_len // nblk
    kern = functools.partial(_dsa_prompt_kernel, tq=tq, ck=ck, cka=cka, n_sel=n_sel,
                             idx_bits=int(s_len).bit_length(), nblk=nblk)
    return pl.pallas_call(
        kern,
        grid=(s_len // tq,),
        in_specs=[pl.BlockSpec((tq, DSA_WIDTH), lambda i: (i, 0)),
                  pl.BlockSpec((tq, IDX_HEADS * IDX_DIM), lambda i: (i, 0)),
                  pl.BlockSpec((tq, LANES), lambda i: (i, 0)),
                  _resident(ikt16.shape), _resident(kt16.shape), _resident(v16.shape)],
        out_specs=pl.BlockSpec((tq, DSA_WIDTH), lambda i: (i, 0)),
        out_shape=jax.ShapeDtypeStruct((s_len, DSA_WIDTH), F32),
        scratch_shapes=[pltpu.VMEM((tq, s_len), I32), pltpu.VMEM((WORD_BITS, tq, n_words), I32),
                        pltpu.VMEM((tq, n_words), I32), pltpu.VMEM((rows, KV_W), BF16),
                        pltpu.VMEM((rows, cka), F32), pltpu.VMEM((rows, cka), F32),
                        pltpu.VMEM((rows, 1), F32), pltpu.VMEM((rows, 1), F32), pltpu.VMEM((rows, KV_W), F32)],
        compiler_params=_cparams(("arbitrary",)),
        name="dsa_prompt",
    )(q16, iq16, iw, ikt16, kt16, v16)


def _softmax_step(s, vals, m_ref, l_ref, acc_ref, vals_keys_on_lanes=False):
    m_prev = m_ref[...]
    m_new = jnp.maximum(m_prev, jnp.max(s, axis=1, keepdims=True))
    alpha = jnp.exp(m_prev - m_new)
    p = jnp.exp(s - m_new)
    l_ref[...] = alpha * l_ref[...] + jnp.sum(p, axis=1, keepdims=True)
    if vals_keys_on_lanes:
        pv = lax.dot_general(p.astype(BF16), vals, _NT, preferred_element_type=F32)
    else:
        pv = jnp.dot(p.astype(BF16), vals, preferred_element_type=F32)
    acc_ref[...] = alpha * acc_ref[...] + pv
    m_ref[...] = m_new


def _softmax_init(m_ref, l_ref, acc_ref):
    m_ref[...] = jnp.full(m_ref.shape, NEG_BIG, F32)
    l_ref[...] = jnp.zeros(l_ref.shape, F32)
    acc_ref[...] = jnp.zeros(acc_ref.shape, F32)


def _mla_prompt_kernel(qcat_ref, kcat_ref, kcatt_ref, wuv_ref, o_ref, q_ref, s0_ref, s1_ref, m_ref, l_ref, acc_ref,
                       *, tq, ck):
    qb = pl.program_id(0)
    n_chunks = ((qb + 1) * tq + ck - 1) // ck
    row_pos = qb * tq + lax.broadcasted_iota(I32, (tq, 1), 0)
    for hh in range(MLA_HEADS):
        q_ref[hh * tq:(hh + 1) * tq, :] = qcat_ref[:, hh * QCAT_W:(hh + 1) * QCAT_W]
    _softmax_init(m_ref, l_ref, acc_ref)
    s_bufs = (s0_ref, s1_ref)

    def scores(c, slot):
        off = pl.multiple_of(c * ck, ck)
        s_bufs[slot][...] = jnp.dot(q_ref[...], kcatt_ref[:, pl.ds(off, ck)], preferred_element_type=F32)

    def consume(c, slot):
        off = pl.multiple_of(c * ck, ck)
        col = off + lax.broadcasted_iota(I32, (tq, ck), 1)
        bias = jnp.where(col <= row_pos, 0.0, NEG_BIG)
        s = s_bufs[slot][...] + jnp.concatenate([bias] * MLA_HEADS, axis=0)
        _softmax_step(s, kcat_ref[pl.ds(off, ck), :MLA_KV_LORA], m_ref, l_ref, acc_ref)

    _pipelined_chunks(n_chunks, scores, consume)
    o_lat = (acc_ref[...] / l_ref[...]).astype(BF16)
    lat = jnp.concatenate([o_lat[hh * tq:(hh + 1) * tq, :] for hh in range(MLA_HEADS)], axis=1)
    o_ref[...] = jnp.dot(lat, wuv_ref[...], preferred_element_type=F32)


def _mla_prompt(qcat, kcat, kcatt, wuv, tq=128, ck=1024):
    s_len = qcat.shape[0]
    ck = min(ck, s_len)
    rows = MLA_HEADS * tq
    return pl.pallas_call(
        functools.partial(_mla_prompt_kernel, tq=tq, ck=ck),
        grid=(s_len // tq,),
        in_specs=[pl.BlockSpec((tq, MLA_HEADS * QCAT_W), lambda i: (i, 0)),
                  _resident(kcat.shape), _resident(kcatt.shape), _resident(wuv.shape)],
        out_specs=pl.BlockSpec((tq, MLA_WIDTH), lambda i: (i, 0)),
        out_shape=jax.ShapeDtypeStruct((s_len, MLA_WIDTH), F32),
        scratch_shapes=[pltpu.VMEM((rows, QCAT_W), BF16), pltpu.VMEM((rows, ck), F32), pltpu.VMEM((rows, ck), F32),
                        pltpu.VMEM((rows, 1), F32), pltpu.VMEM((rows, 1), F32), pltpu.VMEM((rows, MLA_KV_LORA), F32)],
        compiler_params=_cparams(("arbitrary",)),
        name="mla_prompt",
    )(qcat, kcat, kcatt, wuv)


PAGES_PER_STEP = 32


def _pad_rows(x):
    return jnp.concatenate([x, jnp.zeros((PAGE_SIZE - x.shape[0], x.shape[1]), x.dtype)], axis=0)


def _tile_rows(x, n):
    return jnp.concatenate([x] * n, axis=0)


def _idx_scores(iq_rows, iw, keys_t16, t):
    r = jnp.dot(iq_rows, keys_t16, preferred_element_type=F32)
    sc = jnp.zeros((t, keys_t16.shape[1]), F32)
    for h in range(IDX_HEADS):
        sc = sc + iw[:, h:h + 1] * jnp.maximum(r[h * t:(h + 1) * t, :], 0.0)
    return sc


def _sample_idx_mla_kernel(pt_ref, iq_ref, iw_ref, ikn_ref, qcat_ref, ckvn_ref, krn_ref, wuv_ref, *rest,
                           pg, n_pages, n_sel, idx_bits, t):
    idx_refs, ckv_refs, kr_refs = rest[:pg], rest[pg:2 * pg], rest[2 * pg:3 * pg]
    o_ref, bias_ref = rest[3 * pg:3 * pg + 2]
    key_ref, iqr_ref, qrow_ref, m_ref, l_ref, acc_ref = rest[3 * pg + 2:]
    j = pl.program_id(1)
    past = n_pages * PAGE_SIZE
    span = pg * PAGE_SIZE

    @pl.when(j == 0)
    def _init():
        iq = iq_ref[...]
        for h in range(IDX_HEADS):
            iqr_ref[h * t:(h + 1) * t, :] = iq[:, h * IDX_DIM:(h + 1) * IDX_DIM]
        qc = qcat_ref[...]
        for hh in range(MLA_HEADS):
            qrow_ref[hh * t:(hh + 1) * t, :] = qc[:, hh * QCAT_W:(hh + 1) * QCAT_W]
        _softmax_init(m_ref, l_ref, acc_ref)

    iqr = iqr_ref[...].astype(BF16)
    iw = iw_ref[...]
    qrow = qrow_ref[...]
    q_lat = qrow[:, :MLA_KV_LORA].astype(BF16)
    q_rope = qrow[:, MLA_KV_LORA + IDX_DIM:MLA_KV_LORA + IDX_DIM + MLA_ROPE].astype(BF16)

    def mla_scores(ck16, kr_t16):
        return (lax.dot_general(q_lat, ck16, _NT, preferred_element_type=F32)
                + jnp.dot(q_rope, kr_t16, preferred_element_type=F32))

    ikc = jnp.concatenate([r[...] for r in idx_refs], axis=1).astype(BF16)
    ckc = jnp.concatenate([r[...] for r in ckv_refs], axis=0).astype(BF16)
    krc = jnp.concatenate([r[...] for r in kr_refs], axis=1).astype(BF16)
    key_ref[:, pl.ds(pl.multiple_of(j * span, span), span)] = _sortable(_idx_scores(iqr, iw, ikc, t))
    _softmax_step(mla_scores(ckc, krc), ckc, m_ref, l_ref, acc_ref)

    @pl.when(j == n_pages // pg - 1)
    def _finish():
        tok = lax.broadcasted_iota(I32, (t, PAGE_SIZE), 0)
        col = lax.broadcasted_iota(I32, (t, PAGE_SIZE), 1)
        vis = col <= tok
        ikn = _pad_rows(ikn_ref[...]).T.astype(BF16)
        key_ref[:, past:past + PAGE_SIZE] = jnp.where(vis, _sortable(_idx_scores(iqr, iw, ikn, t)), INT_MIN)
        ckn = _pad_rows(ckvn_ref[...]).astype(BF16)
        krn = _pad_rows(krn_ref[...]).T.astype(BF16)
        s = mla_scores(ckn, krn) + _tile_rows(jnp.where(vis, 0.0, NEG_BIG), MLA_HEADS)
        _softmax_step(s, ckn, m_ref, l_ref, acc_ref)
        o_lat = acc_ref[...] / l_ref[...]
        lat = jnp.concatenate([o_lat[hh * t:(hh + 1) * t, :] for hh in range(MLA_HEADS)], axis=1)
        o_ref[...] = jnp.dot(lat.astype(BF16), wuv_ref[...], preferred_element_type=F32)

        keys = key_ref[...]
        cols = lax.broadcasted_iota(I32, keys.shape, 1)

        def count(pred):
            return jnp.sum(jnp.where(pred(keys, cols), 1.0, 0.0), axis=1, keepdims=True)

        thr, jj = _select_threshold(count, n_sel, t, idx_bits)
        sel = (keys > thr) | ((keys == thr) & (cols < jj))
        bias_ref[...] = jnp.where(sel, 0.0, NEG_BIG)


def _page_specs(page_shape, pg, n_pages, page_off):
    return [pl.BlockSpec((None,) + page_shape,
                         functools.partial(lambda s, j, pt, i: (page_off + pt[s * n_pages + j * pg + i], 0, 0), i=i))
            for i in range(pg)]


def _key_major(cache, depth_pool):
    return jnp.moveaxis(cache, 2, -1).reshape(depth_pool, -1, PAGE_SIZE)


def _seq_spec(shape):
    return pl.BlockSpec((None,) + tuple(shape[1:]), lambda s, j, pt: (s, 0, 0))


def _sample_idx_mla(pt_flat, iq, iw, ikn, qcat, ckvn, krn, wuv, c_idx, c_ckv, c_kr, n_pages, page_off=0):
    nseq, t, _ = iq.shape
    pg = min(PAGES_PER_STEP, n_pages)
    past = n_pages * PAGE_SIZE
    lp = past + PAGE_SIZE
    n_sel = min(TOPK_MAX, (past + t) // 4)
    rows = MLA_HEADS * t
    kern = functools.partial(_sample_idx_mla_kernel, pg=pg, n_pages=n_pages, n_sel=n_sel,
                             idx_bits=int(past + t).bit_length(), t=t)
    grid_spec = pltpu.PrefetchScalarGridSpec(
        num_scalar_prefetch=1,
        grid=(nseq, n_pages // pg),
        in_specs=[_seq_spec(iq.shape), _seq_spec(iw.shape), _seq_spec(ikn.shape), _seq_spec(qcat.shape),
                  _seq_spec(ckvn.shape), _seq_spec(krn.shape),
                  pl.BlockSpec(wuv.shape, lambda s, j, pt: (0, 0))]
                 + _page_specs((IDX_DIM, PAGE_SIZE), pg, n_pages, page_off)
                 + _page_specs((PAGE_SIZE, MLA_KV_LORA), pg, n_pages, page_off)
                 + _page_specs((MLA_ROPE, PAGE_SIZE), pg, n_pages, page_off),
        out_specs=[pl.BlockSpec((None, t, MLA_WIDTH), lambda s, j, pt: (s, 0, 0)),
                   pl.BlockSpec((None, t, lp), lambda s, j, pt: (s, 0, 0))],
        scratch_shapes=[pltpu.VMEM((t, lp), I32), pltpu.VMEM((IDX_HEADS * t, IDX_DIM), F32),
                        pltpu.VMEM((rows, QCAT_W), F32), pltpu.VMEM((rows, 1), F32),
                        pltpu.VMEM((rows, 1), F32), pltpu.VMEM((rows, MLA_KV_LORA), F32)],
    )
    return pl.pallas_call(
        kern, grid_spec=grid_spec,
        out_shape=[jax.ShapeDtypeStruct((nseq, t, MLA_WIDTH), F32), jax.ShapeDtypeStruct((nseq, t, lp), F32)],
        compiler_params=_cparams(("arbitrary", "arbitrary")),
        name="sample_idx_mla",
    )(pt_flat, iq, iw, ikn, qcat, ckvn, krn, wuv, *([c_idx] * pg), *([c_ckv] * pg), *([c_kr] * pg))


def _sample_dsa_kernel(pt_ref, q_ref, kn_ref, vn_ref, bias_ref, biasn_ref, *rest, pg, n_pages, t):
    k_refs, v_refs = rest[:pg], rest[pg:2 * pg]
    o_ref = rest[2 * pg]
    qbd_ref, m_ref, l_ref, acc_ref = rest[2 * pg + 1:]
    j = pl.program_id(1)
    lane_g = lax.broadcasted_iota(I32, (t, KV_W), 1) // HEAD_DIM

    @pl.when(j == 0)
    def _init():
        q = q_ref[...]
        for r in range(DSA_GROUP):
            qr = q[:, r * KV_W:(r + 1) * KV_W]
            for g in range(DSA_KV_HEADS):
                hh = r * DSA_KV_HEADS + g
                qbd_ref[hh * t:(hh + 1) * t, :] = jnp.where(lane_g == g, qr, 0.0)
        _softmax_init(m_ref, l_ref, acc_ref)

    qbd = qbd_ref[...].astype(BF16)
    kc = jnp.concatenate([r[...] for r in k_refs], axis=1).astype(BF16)
    vc = jnp.concatenate([r[...] for r in v_refs], axis=1).astype(BF16)
    s = jnp.dot(qbd, kc, preferred_element_type=F32) + _tile_rows(bias_ref[...], DSA_HEADS)
    _softmax_step(s, vc, m_ref, l_ref, acc_ref, vals_keys_on_lanes=True)

    @pl.when(j == n_pages // pg - 1)
    def _finish():
        kn = _pad_rows(kn_ref[...]).astype(BF16)
        vn = _pad_rows(vn_ref[...]).astype(BF16)
        s = lax.dot_general(qbd, kn, _NT, preferred_element_type=F32) + _tile_rows(biasn_ref[...], DSA_HEADS)
        _softmax_step(s, vn, m_ref, l_ref, acc_ref)
        o_full = acc_ref[...] / l_ref[...]
        for r in range(DSA_GROUP):
            o_r = jnp.zeros((t, KV_W), F32)
            for g in range(DSA_KV_HEADS):
                hh = r * DSA_KV_HEADS + g
                o_r = o_r + jnp.where(lane_g == g, o_full[hh * t:(hh + 1) * t, :], 0.0)
            o_ref[:, r * KV_W:(r + 1) * KV_W] = o_r


def _sample_dsa(pt_flat, q, kn, vn, bias, c_k, c_v, n_pages, page_off=0):
    nseq, t, _ = q.shape
    pg = min(PAGES_PER_STEP, n_pages)
    rows = DSA_HEADS * t
    span = pg * PAGE_SIZE
    grid_spec = pltpu.PrefetchScalarGridSpec(
        num_scalar_prefetch=1,
        grid=(nseq, n_pages // pg),
        in_specs=[_seq_spec(q.shape), _seq_spec(kn.shape), _seq_spec(vn.shape),
                  pl.BlockSpec((None, t, span), lambda s, j, pt: (s, 0, j)),
                  pl.BlockSpec((None, t, PAGE_SIZE), lambda s, j, pt: (s, 0, n_pages))]
                 + _page_specs((KV_W, PAGE_SIZE), pg, n_pages, page_off)
                 + _page_specs((KV_W, PAGE_SIZE), pg, n_pages, page_off),
        out_specs=pl.BlockSpec((None, t, DSA_WIDTH), lambda s, j, pt: (s, 0, 0)),
        scratch_shapes=[pltpu.VMEM((rows, KV_W), F32), pltpu.VMEM((rows, 1), F32),
                        pltpu.VMEM((rows, 1), F32), pltpu.VMEM((rows, KV_W), F32)],
    )
    return pl.pallas_call(
        functools.partial(_sample_dsa_kernel, pg=pg, n_pages=n_pages, t=t), grid_spec=grid_spec,
        out_shape=jax.ShapeDtypeStruct((nseq, t, DSA_WIDTH), F32),
        compiler_params=_cparams(("arbitrary", "arbitrary")),
        name="sample_dsa",
    )(pt_flat, q, kn, vn, bias, bias, *([c_k] * pg), *([c_v] * pg))


def _merge_kernel(x_ref, od_ref, om_ref, gate_ref, gd_ref, gm_ref, wo_ref, o_ref):
    a = jnp.concatenate([_rms(od_ref[...]) * gd_ref[...], _rms(om_ref[...]) * gm_ref[...]], axis=1)
    y = jnp.dot(a.astype(BF16), wo_ref[...], preferred_element_type=F32)
    o_ref[...] = x_ref[...] + gate_ref[...] * y


def _mod_spec(a, tm):
    if a.shape[0] == 1:
        return pl.BlockSpec((1, a.shape[1]), lambda i, *_: (0, 0))
    return pl.BlockSpec((tm, a.shape[1]), lambda i, *_: (i, 0))


def _merge(x, o_dsa, o_mla, gate, g_dsa, g_mla, w_out16, tm=256):
    t = x.shape[0]
    tm = min(tm, t)

    def row(w):
        return pl.BlockSpec((tm, w), lambda i: (i, 0))

    return pl.pallas_call(
        _merge_kernel,
        grid=(t // tm,),
        in_specs=[row(D_MODEL), row(DSA_WIDTH), row(MLA_WIDTH), _mod_spec(gate, tm),
                  _full_spec((1, DSA_WIDTH)), _full_spec((1, MLA_WIDTH)), _full_spec(w_out16.shape)],
        out_specs=row(D_MODEL),
        out_shape=jax.ShapeDtypeStruct((t, D_MODEL), F32),
        compiler_params=_cparams(("arbitrary",)),
        name="merge",
    )(x, o_dsa, o_mla, gate, g_dsa.reshape(1, -1), g_mla.reshape(1, -1), w_out16)


EXPERT_CHUNK = 2048
I_PER_CHUNK = EXPERT_CHUNK // PEER_NKEYS


def _erf(x):
    return lax.erf(x)


def _extract_top(cur, n):
    vals = []
    for _ in range(n):
        mx = jnp.max(cur, axis=0, keepdims=True)
        vals.append(mx)
        cur = jnp.where(cur == mx, -jnp.inf, cur)
    return vals


def _peer_kernel(x1_ref, shift_ref, scale_ref, gate_ref, gffn_ref, gfin_ref, wpq_ref, keys_ref, u_ref, vt_ref,
                 y_ref, ht_ref, qry_ref, s0_ref, s1_ref, e0_ref, e1_ref, tau_ref, act_ref, ga_ref, acc_ref,
                 *, tm, final_norm):
    c = pl.program_id(1)

    @pl.when(c == 0)
    def _prep():
        h = _rms(x1_ref[...]) * gffn_ref[...] * (1.0 + scale_ref[...]) + shift_ref[...]
        ht_ref[...] = h.T.astype(BF16)
        qry_ref[...] = jnp.dot(wpq_ref[...], ht_ref[...], preferred_element_type=F32)
        acc_ref[...] = jnp.zeros(acc_ref.shape, F32)

        def head(hh, carry):
            tops = []
            for p, s_ref in ((0, s0_ref), (1, s1_ref)):
                blk = pl.multiple_of((hh * 2 + p) * PEER_HALF, PEER_HALF)
                sub = jnp.dot(keys_ref[pl.ds(blk, PEER_NKEYS), :], qry_ref[pl.ds(blk, PEER_HALF), :],
                              preferred_element_type=F32, precision=lax.Precision.HIGHEST)
                s_ref[hh] = sub
                tops.append(_extract_top(sub, PEER_TOPK))
            top0, top1 = tops
            t1 = jnp.concatenate(top1, axis=0)
            cand = jnp.concatenate([top0[a] + t1 for a in range(PEER_TOPK)], axis=0)
            tau = _extract_top(cand, PEER_TOPK)[-1]
            cmax = top0[0] + top1[0]
            z = jnp.sum(jnp.where(cand >= tau, jnp.exp(cand - cmax), 0.0), axis=0, keepdims=True)
            e0_ref[hh] = jnp.exp(s0_ref[hh] - top0[0]) / z
            e1_ref[hh] = jnp.exp(s1_ref[hh] - top1[0])
            tau_ref[pl.ds(hh, 1), :] = tau
            return carry

        lax.fori_loop(0, PEER_HEADS, head, 0)

    a = jnp.dot(u_ref[...], ht_ref[...], preferred_element_type=F32)
    act_ref[...] = 0.5 * a * (1.0 + _erf(a * (2.0 ** -0.5)))

    def i_slab(il, carry):
        i = c * I_PER_CHUNK + il
        g = jnp.zeros((PEER_NKEYS, tm), F32)
        for hh in range(PEER_HEADS):
            tsum = s0_ref[hh, pl.ds(i, 1), :] + s1_ref[hh]
            hit = tsum >= tau_ref[hh:hh + 1, :]
            g = g + jnp.where(hit, e1_ref[hh], 0.0) * e0_ref[hh, pl.ds(i, 1), :]
        rows = pl.ds(pl.multiple_of(il * PEER_NKEYS, PEER_NKEYS), PEER_NKEYS)
        ga_ref[rows, :] = (g * act_ref[rows, :]).astype(BF16)
        return carry

    lax.fori_loop(0, I_PER_CHUNK, i_slab, 0)
    acc_ref[...] += jnp.dot(vt_ref[...], ga_ref[...], preferred_element_type=F32)

    @pl.when(c == pl.num_programs(1) - 1)
    def _finish():
        x2 = x1_ref[...] + gate_ref[...] * acc_ref[...].T
        y_ref[...] = _rms(x2) * gfin_ref[...] if final_norm else x2


def _peer_final(x1, shift, scale, gate, g_ffn, g_final, wpq_t16, keys2d, u16, vt16, final_norm, tm=256):
    t = x1.shape[0]
    tm = min(tm, t)
    n_chunks = N_EXPERTS // EXPERT_CHUNK
    hshape = (PEER_HEADS, PEER_NKEYS, tm)
    return pl.pallas_call(
        functools.partial(_peer_kernel, tm=tm, final_norm=final_norm),
        grid=(t // tm, n_chunks),
        in_specs=[pl.BlockSpec((tm, D_MODEL), lambda i, c: (i, 0)),
                  _mod_spec(shift, tm), _mod_spec(scale, tm), _mod_spec(gate, tm),
                  _full_spec((1, D_MODEL)), _full_spec((1, D_MODEL)),
                  _resident(wpq_t16.shape), _resident(keys2d.shape),
                  pl.BlockSpec((EXPERT_CHUNK, D_MODEL), lambda i, c: (c, 0)),
                  pl.BlockSpec((D_MODEL, EXPERT_CHUNK), lambda i, c: (0, c))],
        out_specs=pl.BlockSpec((tm, D_MODEL), lambda i, c: (i, 0)),
        out_shape=jax.ShapeDtypeStruct((t, D_MODEL), F32),
        scratch_shapes=[pltpu.VMEM((D_MODEL, tm), BF16), pltpu.VMEM((2 * PEER_HEADS * PEER_HALF, tm), F32),
                        pltpu.VMEM(hshape, F32), pltpu.VMEM(hshape, F32), pltpu.VMEM(hshape, F32),
                        pltpu.VMEM(hshape, F32), pltpu.VMEM((PEER_HEADS, tm), F32),
                        pltpu.VMEM((EXPERT_CHUNK, tm), F32), pltpu.VMEM((EXPERT_CHUNK, tm), BF16),
                        pltpu.VMEM((D_MODEL, tm), F32)],
        compiler_params=_cparams(("arbitrary", "arbitrary")),
        name="peer_final",
    )(x1, shift, scale, gate, g_ffn.reshape(1, -1), g_final.reshape(1, -1), wpq_t16, keys2d, u16, vt16)


def _mod_rows(m, k, n_rep):
    rows = m[:, k * D_MODEL:(k + 1) * D_MODEL]
    return rows if rows.shape[0] == 1 else jnp.repeat(rows, n_rep, axis=0)


def _perm_heads(a):
    tail = a.shape[1:]
    return a.reshape((DSA_KV_HEADS, DSA_GROUP, HEAD_DIM) + tail).swapaxes(0, 1).reshape((DSA_WIDTH,) + tail)


def kernel(x_prompt, x_sample, cache_dsa_k, cache_dsa_v, cache_idx_k, cache_mla_ckv, cache_mla_krope, page_table, c_prompt, c_sample, w_mod, b_mod, g_mix, g_ffn, w_in, g_cq, g_ckv, w_uq, w_uk, w_uv, g_out_dsa, g_out_mla, w_out, w_pq, peer_keys, peer_u, peer_v, g_final):
    depth = w_mod.shape[0]
    b, s_len, _ = x_prompt.shape
    nseq, t, _ = x_sample.shape
    n_pages = page_table.shape[1]
    past = n_pages * PAGE_SIZE
    n_pool = cache_dsa_k.shape[1]
    xp = x_prompt.reshape(b * s_len, D_MODEL)
    xs = x_sample.reshape(nseq * t, D_MODEL)
    cos_p, sin_p = _rope_tables(jnp.tile(jnp.arange(s_len), b))
    cos_s, sin_s = _rope_tables(jnp.tile(past + jnp.arange(t), nseq))
    pt_flat = page_table.reshape(-1)
    c_k = _key_major(cache_dsa_k, depth * n_pool)
    c_v = _key_major(cache_dsa_v, depth * n_pool)
    c_idx = _key_major(cache_idx_k, depth * n_pool)
    c_kr = _key_major(cache_mla_krope, depth * n_pool)
    c_ckv = cache_mla_ckv.reshape(depth * n_pool, PAGE_SIZE, MLA_KV_LORA)
    new_p = [[] for _ in range(5)]
    new_s = [[] for _ in range(5)]

    for l in range(depth):
        last = l == depth - 1
        mod = _adaln(jnp.concatenate([c_prompt, c_sample], axis=0), w_mod[l], b_mod[l])
        mp, ms = mod[:b], mod[b:]
        wqn, wqr, wqrr, wuk, wuv = _pack_mla_weights(w_uq[l], w_uk[l], w_uv[l])
        packed = _pack_in_weights(w_in[l]) + (wqn, wqr, wqrr, wuk)
        g_dsa = _perm_heads(g_out_dsa[l])
        w_out16 = jnp.concatenate([_perm_heads(w_out[l][:DSA_WIDTH]), w_out[l][DSA_WIDTH:]], axis=0).astype(BF16)
        peer_w = (w_pq[l].T.astype(BF16), peer_keys[l].reshape(2 * PEER_HEADS * PEER_NKEYS, PEER_HALF),
                  peer_u[l].astype(BF16), peer_v[l].T.astype(BF16))

        (k32, v32, ik32, ckv32, kr32, q16, iq16, iw, v16, kcat, qcat, kt16, ikt16, kcatt) = _project(
            xp, _mod_rows(mp, 0, s_len), _mod_rows(mp, 1, s_len), g_mix[l], packed, cos_p, sin_p, g_cq[l], g_ckv[l])
        o_dsa, o_mla = [], []
        for bi in range(b):
            sl = slice(bi * s_len, (bi + 1) * s_len)
            o_dsa.append(_dsa_prompt(q16[sl], iq16[sl], iw[sl], ikt16[:, sl], kt16[:, sl], v16[sl]))
            o_mla.append(_mla_prompt(qcat[sl], kcat[sl], kcatt[:, sl], wuv))
        o_dsa = o_dsa[0] if b == 1 else jnp.concatenate(o_dsa, axis=0)
        o_mla = o_mla[0] if b == 1 else jnp.concatenate(o_mla, axis=0)
        x1 = _merge(xp, o_dsa, o_mla, _mod_rows(mp, 2, s_len), g_dsa, g_out_mla[l], w_out16)
        xp = _peer_final(x1, _mod_rows(mp, 3, s_len), _mod_rows(mp, 4, s_len), _mod_rows(mp, 5, s_len),
                         g_ffn[l], g_final, *peer_w, final_norm=last)
        for lst, (val, shp) in zip(new_p, ((k32, (DSA_KV_HEADS, HEAD_DIM)), (v32, (DSA_KV_HEADS, HEAD_DIM)),
                                           (ik32, (IDX_DIM,)), (ckv32, (MLA_KV_LORA,)), (kr32, (MLA_ROPE,)))):
            lst.append(val.reshape((b, s_len) + shp))

        ms_tok = [jnp.repeat(ms[:, k * D_MODEL:(k + 1) * D_MODEL], t, axis=0) for k in range(6)]
        (k32, v32, ik32, ckv32, kr32, q16, iq16, iw, v16, kcat, qcat, kt16, ikt16, kcatt) = _project(
            xs, ms_tok[0], ms_tok[1], g_mix[l], packed, cos_s, sin_s, g_cq[l], g_ckv[l], tm=min(256, nseq * t))

        def seq3(a):
            return a.reshape(nseq, t, a.shape[-1]).astype(F32)

        o_mla, bias = _sample_idx_mla(pt_flat, seq3(iq16), seq3(iw), seq3(ik32), seq3(qcat), seq3(ckv32),
                                      seq3(kr32), wuv, c_idx, c_ckv, c_kr, n_pages, page_off=l * n_pool)
        o_dsa = _sample_dsa(pt_flat, seq3(q16), seq3(k32), seq3(v32), bias, c_k, c_v, n_pages,
                            page_off=l * n_pool)
        x1 = _merge(xs, o_dsa.reshape(nseq * t, DSA_WIDTH), o_mla.reshape(nseq * t, MLA_WIDTH), ms_tok[2],
                    g_dsa, g_out_mla[l], w_out16)
        xs = _peer_final(x1, ms_tok[3], ms_tok[4], ms_tok[5], g_ffn[l], g_final, *peer_w, final_norm=last)
        for lst, (val, shp) in zip(new_s, ((k32, (DSA_KV_HEADS, HEAD_DIM)), (v32, (DSA_KV_HEADS, HEAD_DIM)),
                                           (ik32, (IDX_DIM,)), (ckv32, (MLA_KV_LORA,)), (kr32, (MLA_ROPE,)))):
            lst.append(val.reshape((nseq, t) + shp))

    sp = [jnp.stack(a, axis=0) for a in new_p]
    ss = [jnp.stack(a, axis=0) for a in new_s]
    return (xp.reshape(b, s_len, D_MODEL), xs.reshape(nseq, t, D_MODEL),
            sp[0], sp[1], sp[2], sp[3], sp[4], ss[0], ss[1], ss[2], ss[3], ss[4])
```

```python
import functools

import numpy as np
import jax
import jax.numpy as jnp
from jax import lax
from jax.experimental import pallas as pl
from jax.experimental.pallas import tpu as pltpu

F32 = jnp.float32
BF16 = jnp.bfloat16
I32 = jnp.int32

D_MODEL = 1024
HEAD_DIM = 64
DSA_WIDTH = 512
DSA_HEADS = 8
DSA_KV_HEADS = 4
DSA_GROUP = 2
KV_W = DSA_KV_HEADS * HEAD_DIM
IDX_HEADS = 4
IDX_DIM = 64
TOPK_MAX = 256
MLA_V = 64
MLA_WIDTH = 512
MLA_HEADS = 8
MLA_NOPE = 64
MLA_ROPE = 32
MLA_Q_LORA = 384
MLA_KV_LORA = 256
MLA_SCALE = (MLA_NOPE + MLA_ROPE) ** -0.5
PEER_HEADS = 8
PEER_NKEYS = 128
N_EXPERTS = PEER_NKEYS * PEER_NKEYS
PEER_HALF = 128
PEER_TOPK = 16
PAGE_SIZE = 128
ROPE_THETA = 10000.0
EPS = 1e-6
IN_SPLITS = (DSA_WIDTH, KV_W, KV_W, IDX_HEADS * IDX_DIM, IDX_HEADS, IDX_DIM,
             MLA_Q_LORA, MLA_KV_LORA, MLA_ROPE)

LANES = 128
SUBLANES = 8
VMEM_LIMIT = 56 * 1024 * 1024

C_Q, C_K, C_IQ, C_D, C_V, C_CKV, C_CQ, C_IW, C_END = 0, 512, 768, 1024, 1152, 1408, 1664, 2048, 2176
ROT_W = C_V
QCAT_W = 384
INT_MIN = -2147483648
INT_MAX = 2147483647
NEG_BIG = -1e30


def _cparams(sem):
    return pltpu.CompilerParams(dimension_semantics=sem, vmem_limit_bytes=VMEM_LIMIT)


def _full_spec(shape):
    nd = len(shape)
    return pl.BlockSpec(shape, lambda *_: (0,) * nd)


def _adaln_kernel(c_ref, w_ref, b_ref, o_ref):
    c = c_ref[...]
    a = c / (1.0 + jnp.exp(-c))
    o_ref[...] = jnp.dot(a, w_ref[...], preferred_element_type=F32,
                         precision=lax.Precision.HIGHEST) + b_ref[...]


def _adaln(c, w_mod, b_mod):
    n = c.shape[0]
    n_pad = -(-n // SUBLANES) * SUBLANES
    cp = jnp.pad(c, ((0, n_pad - n), (0, 0)))
    tn = 1536
    out = pl.pallas_call(
        _adaln_kernel,
        grid=(w_mod.shape[1] // tn,),
        in_specs=[pl.BlockSpec((n_pad, D_MODEL), lambda j: (0, 0)),
                  pl.BlockSpec((D_MODEL, tn), lambda j: (0, j)),
                  pl.BlockSpec((1, tn), lambda j: (0, j))],
        out_specs=pl.BlockSpec((n_pad, tn), lambda j: (0, j)),
        out_shape=jax.ShapeDtypeStruct((n_pad, w_mod.shape[1]), F32),
        compiler_params=_cparams(("arbitrary",)),
        name="adaln",
    )(cp, w_mod, b_mod.reshape(1, -1))
    return out[:n]


def _swap_halves(w, hd):
    d, n = w.shape
    w4 = w.reshape(d, n // hd, 2, hd // 2)
    return w4[:, :, ::-1, :].reshape(d, n)


def _pack_in_weights(w_in):
    cuts = [int(v) for v in np.cumsum(IN_SPLITS)[:-1]]
    wq, wk, wv, wiq, wiw, wik, wcq, wckv, wkr = jnp.split(w_in, cuts, axis=1)
    wq = wq.reshape(D_MODEL, DSA_KV_HEADS, DSA_GROUP, HEAD_DIM).transpose(0, 2, 1, 3).reshape(D_MODEL, DSA_WIDTH)
    pad32 = jnp.zeros((D_MODEL, 32), F32)
    w_d = jnp.concatenate([wik, wkr, pad32], axis=1)
    w_d_rot = jnp.concatenate([_swap_halves(wik, IDX_DIM), _swap_halves(wkr, MLA_ROPE), pad32], axis=1)
    wiw_p = jnp.pad(wiw, ((0, 0), (0, LANES - IDX_HEADS)))
    w_main = jnp.concatenate([wq, wk, wiq, w_d, wv, wckv, wcq, wiw_p], axis=1)
    w_rot = jnp.concatenate([_swap_halves(wq, HEAD_DIM), _swap_halves(wk, HEAD_DIM),
                             _swap_halves(wiq, IDX_DIM), w_d_rot], axis=1)
    return w_main.astype(BF16), w_rot.astype(BF16)


def _pack_mla_weights(w_uq, w_uk, w_uv):
    w_n = w_uq[:, :, :MLA_NOPE].reshape(MLA_Q_LORA, MLA_HEADS * MLA_NOPE)
    w_r = w_uq[:, :, MLA_NOPE:]
    w_r_rot = jnp.concatenate([w_r[..., MLA_ROPE // 2:], w_r[..., :MLA_ROPE // 2]], axis=-1)

    def place(w):
        z = jnp.zeros((MLA_Q_LORA, MLA_HEADS, LANES), F32)
        return z.at[:, :, IDX_DIM:IDX_DIM + MLA_ROPE].set(w).reshape(MLA_Q_LORA, MLA_HEADS * LANES)

    eye = jnp.eye(MLA_HEADS, dtype=F32)
    uk = jnp.einsum('chd,hk->hdkc', w_uk, eye).reshape(MLA_HEADS * MLA_NOPE, MLA_HEADS * MLA_KV_LORA)
    uv = jnp.einsum('chd,hk->hckd', w_uv, eye).reshape(MLA_HEADS * MLA_KV_LORA, MLA_HEADS * MLA_V)
    return (w_n.astype(BF16), place(w_r).astype(BF16), place(w_r_rot).astype(BF16),
            uk.astype(BF16), uv.astype(BF16))


def _rope_tables(pos):
    pos = pos.astype(F32)[:, None]

    def cs(dim):
        half = dim // 2
        inv = ROPE_THETA ** (-jnp.arange(half, dtype=F32) / half)
        ang = pos * inv[None, :]
        c, s = jnp.cos(ang), jnp.sin(ang)
        return jnp.concatenate([c, c], axis=1), jnp.concatenate([-s, s], axis=1)

    c64, s64 = cs(HEAD_DIM)
    c32, s32 = cs(MLA_ROPE)
    z = jnp.zeros((pos.shape[0], 32), F32)
    return (jnp.concatenate([c64, c64, c64, c32, z], axis=1),
            jnp.concatenate([s64, s64, s64, s32, z], axis=1))


def _rms(x):
    return x * lax.rsqrt(jnp.mean(x * x, axis=-1, keepdims=True) + EPS)


def _proj_kernel(x_ref, shift_ref, scale_ref, gmix_ref, wm_ref, wr_ref, cos_ref, sin_ref,
                 gcq_ref, gckv_ref, wqn_ref, wqr_ref, wqrr_ref, wuk_ref,
                 k32_ref, v32_ref, ik32_ref, ckv32_ref, kr32_ref,
                 q16_ref, iq16_ref, iw_ref, v16_ref, kcat_ref, qcat_ref, kt16_ref, ikt16_ref, kcatt_ref):
    x = x_ref[...]
    h = _rms(x) * gmix_ref[...] * (1.0 + scale_ref[...]) + shift_ref[...]
    hb = h.astype(BF16)
    main = jnp.dot(hb, wm_ref[...], preferred_element_type=F32)
    rot = jnp.dot(hb, wr_ref[...], preferred_element_type=F32)
    cos = cos_ref[...]
    sin = sin_ref[...]
    cos_h, sin_h = cos[:, :LANES], sin[:, :LANES]
    cos_d, sin_d = cos[:, LANES:], sin[:, LANES:]

    def rope(lo, hi):
        reps = (hi - lo) // LANES
        c = jnp.concatenate([cos_h] * reps, axis=1)
        s = jnp.concatenate([sin_h] * reps, axis=1)
        return main[:, lo:hi] * c + rot[:, lo:hi] * s

    q = rope(C_Q, C_K)
    k = rope(C_K, C_IQ)
    iq = rope(C_IQ, C_D)
    sec_d = main[:, C_D:C_V] * cos_d + rot[:, C_D:C_V] * sin_d
    v = main[:, C_V:C_CKV]
    ckv = _rms(main[:, C_CKV:C_CQ]) * gckv_ref[...]
    cqn = (_rms(main[:, C_CQ:C_IW]) * gcq_ref[...]).astype(BF16)

    k32_ref[...] = k
    v32_ref[...] = v
    ik32_ref[...] = sec_d[:, :IDX_DIM]
    ckv32_ref[...] = ckv
    kr32_ref[...] = sec_d[:, IDX_DIM:IDX_DIM + MLA_ROPE]
    q16_ref[...] = (q * HEAD_DIM ** -0.5).astype(BF16)
    iq16_ref[...] = iq.astype(BF16)
    iw_ref[...] = main[:, C_IW:C_END] * (IDX_HEADS ** -0.5 * IDX_DIM ** -0.5)
    v16_ref[...] = v.astype(BF16)
    lane = lax.broadcasted_iota(I32, sec_d.shape, 1)
    kr_blk = jnp.where((lane >= IDX_DIM) & (lane < IDX_DIM + MLA_ROPE), sec_d, 0.0)
    kcat_ref[...] = jnp.concatenate([ckv, kr_blk], axis=1).astype(BF16)
    kt16_ref[...] = k.T.astype(BF16)
    ikt16_ref[...] = sec_d.T[:IDX_DIM, :].astype(BF16)
    kcatt_ref[...] = jnp.concatenate([ckv.T, kr_blk.T], axis=0).astype(BF16)

    qn = jnp.dot(cqn, wqn_ref[...], preferred_element_type=F32).astype(BF16)
    qlat = jnp.dot(qn, wuk_ref[...], preferred_element_type=F32) * MLA_SCALE
    qr_m = jnp.dot(cqn, wqr_ref[...], preferred_element_type=F32)
    qr_r = jnp.dot(cqn, wqrr_ref[...], preferred_element_type=F32)
    cd = jnp.concatenate([cos_d] * MLA_HEADS, axis=1)
    sd = jnp.concatenate([sin_d] * MLA_HEADS, axis=1)
    qr = (qr_m * cd + qr_r * sd) * MLA_SCALE
    for hh in range(MLA_HEADS):
        qcat_ref[:, hh * QCAT_W:hh * QCAT_W + MLA_KV_LORA] = (
            qlat[:, hh * MLA_KV_LORA:(hh + 1) * MLA_KV_LORA].astype(BF16))
        qcat_ref[:, hh * QCAT_W + MLA_KV_LORA:(hh + 1) * QCAT_W] = (
            qr[:, hh * LANES:(hh + 1) * LANES].astype(BF16))


def _project(x, shift, scale, g_mix, packed, cos_t, sin_t, g_cq, g_ckv, tm=256):
    t = x.shape[0]
    w_main, w_rot, wqn, wqr, wqrr, wuk = packed
    per_tok = shift.shape[0] != 1
    mod_spec = pl.BlockSpec((tm, D_MODEL), lambda i: (i, 0)) if per_tok else _full_spec((1, D_MODEL))

    def row(w):
        return pl.BlockSpec((tm, w), lambda i: (i, 0))

    widths = [(KV_W, F32), (KV_W, F32), (IDX_DIM, F32), (MLA_KV_LORA, F32), (MLA_ROPE, F32),
              (DSA_WIDTH, BF16), (IDX_HEADS * IDX_DIM, BF16), (LANES, F32), (KV_W, BF16),
              (QCAT_W, BF16), (MLA_HEADS * QCAT_W, BF16)]
    heights = [KV_W, IDX_DIM, QCAT_W]
    return pl.pallas_call(
        _proj_kernel,
        grid=(t // tm,),
        in_specs=[row(D_MODEL), mod_spec, mod_spec, _full_spec((1, D_MODEL)),
                  _full_spec(w_main.shape), _full_spec(w_rot.shape), row(2 * LANES), row(2 * LANES),
                  _full_spec((1, MLA_Q_LORA)), _full_spec((1, MLA_KV_LORA)),
                  _full_spec(wqn.shape), _full_spec(wqr.shape), _full_spec(wqrr.shape), _full_spec(wuk.shape)],
        out_specs=[row(w) for w, _ in widths] + [pl.BlockSpec((h, tm), lambda i: (0, i)) for h in heights],
        out_shape=[jax.ShapeDtypeStruct((t, w), dt) for w, dt in widths]
                  + [jax.ShapeDtypeStruct((h, t), BF16) for h in heights],
        compiler_params=_cparams(("arbitrary",)),
        name="project",
    )(x, shift, scale, g_mix.reshape(1, -1), w_main, w_rot, cos_t, sin_t,
      g_cq.reshape(1, -1), g_ckv.reshape(1, -1), wqn, wqr, wqrr, wuk)


def _sortable(s):
    b = pltpu.bitcast(s + 0.0, I32)
    return b ^ ((b >> 31) & INT_MAX)


_NT = (((1,), (1,)), ((), ()))


def _fold_lanes(x):
    part = x[:, :LANES]
    for j in range(1, x.shape[1] // LANES):
        part = part + x[:, j * LANES:(j + 1) * LANES]
    return part


def _select_threshold(count, n_sel, rows, idx_bits):
    def bit_step(i, t):
        cand = t ^ lax.shift_left(jnp.int32(1), 31 - i)
        return jnp.where(count(lambda k, c: k >= cand) >= n_sel, cand, t)

    t = lax.fori_loop(0, 32, bit_step, jnp.full((rows, 1), INT_MIN, I32))
    need = n_sel - count(lambda k, c: k > t)
    n_eq = count(lambda k, c: k == t)
    return t, _resolve_ties(count, t, need, n_eq, rows, idx_bits)


def _resolve_ties(count, t, need, n_eq, rows, idx_bits):
    live = t != INT_MIN
    tie = jnp.max(jnp.where((n_eq > need) & live, 1.0, 0.0)) > 0.0

    def tie_search():
        def step(i, j):
            cand = j | lax.shift_left(jnp.int32(1), idx_bits - 1 - i)
            return jnp.where(count(lambda k, c: (k == t) & (c < cand)) <= need, cand, j)
        return lax.fori_loop(0, idx_bits, step, jnp.zeros((rows, 1), I32))

    j = lax.cond(tie, tie_search, lambda: jnp.full((rows, 1), INT_MAX, I32))
    return jnp.where(live, j, 0)


_TRANSPOSE_MASKS = ((16, 0x0000FFFF), (8, 0x00FF00FF), (4, 0x0F0F0F0F), (2, 0x33333333), (1, 0x55555555))
WORD_BITS = 32


def _bit_transpose32(a):
    a = list(a)
    for sh, mask in _TRANSPOSE_MASKS:
        for k in range(WORD_BITS):
            if k & sh:
                continue
            t = (a[k] ^ lax.shift_right_logical(a[k + sh], jnp.int32(sh))) & mask
            a[k] = a[k] ^ t
            a[k + sh] = a[k + sh] ^ lax.shift_left(t, jnp.int32(sh))
    return a


def _popcount_rows(x):
    return jnp.sum(_fold_lanes(lax.population_count(x)).astype(F32), axis=1, keepdims=True)


def _pipelined_chunks(n_chunks, scores, consume):
    scores(0, 0)

    def pair(i, carry):
        c = 2 * i
        scores(c + 1, 1)
        consume(c, 0)
        scores(jnp.minimum(c + 2, n_chunks - 1), 0)
        consume(c + 1, 1)
        return carry

    lax.fori_loop(0, n_chunks // 2, pair, 0)

    @pl.when(n_chunks % 2 == 1)
    def _():
        consume(n_chunks - 1, 0)


def _dsa_prompt_kernel(q_ref, iq_ref, iw_ref, ikt_ref, kt_ref, v_ref, o_ref,
                       key_ref, plane_ref, alive_ref, qbd_ref, s0_ref, s1_ref, m_ref, l_ref, acc_ref,
                       *, tq, ck, cka, n_sel, idx_bits, nblk):
    qb = pl.program_id(0)
    n_att = ((qb + 1) * tq + cka - 1) // cka
    n_chunks = n_att * (cka // ck)
    row_pos = qb * tq + lax.broadcasted_iota(I32, (tq, 1), 0)
    iw = iw_ref[...]
    iq = iq_ref[...]
    iq_h = [iq[:, h * IDX_DIM:(h + 1) * IDX_DIM] for h in range(IDX_HEADS)]
    iw_h = [iw[:, h:h + 1] for h in range(IDX_HEADS)]

    def cols(off):
        return off + lax.broadcasted_iota(I32, (tq, ck), 1)

    def score_chunk(c, carry):
        off = pl.multiple_of(c * ck, ck)
        ikc = ikt_ref[:, pl.ds(off, ck)]
        sc = jnp.zeros((tq, ck), F32)
        for h in range(IDX_HEADS):
            r = jnp.dot(iq_h[h], ikc, preferred_element_type=F32)
            sc = sc + iw_h[h] * jnp.maximum(r, 0.0)
        key_ref[:, pl.ds(off, ck)] = jnp.where(cols(off) <= row_pos, _sortable(sc), INT_MIN)
        return carry

    lax.fori_loop(0, n_chunks, score_chunk, 0)

    def count(pred):
        def body(c, cnt):
            off = pl.multiple_of(c * ck, ck)
            hit = pred(key_ref[:, pl.ds(off, ck)], cols(off))
            return cnt + _fold_lanes(jnp.where(hit, 1.0, 0.0))
        cnt = lax.fori_loop(0, n_chunks, body, jnp.zeros((tq, LANES), F32))
        return jnp.sum(cnt, axis=1, keepdims=True)

    group = nblk * LANES
    n_groups = (n_chunks * ck + group - 1) // group

    def fill(c, carry):
        key_ref[:, pl.ds(pl.multiple_of(c * ck, ck), ck)] = jnp.full((tq, ck), INT_MIN, I32)
        return carry

    lax.fori_loop(n_chunks, n_groups * (group // ck), fill, 0)
    pad = [jnp.full((SUBLANES, LANES), INT_MIN, I32)] * (WORD_BITS - nblk)

    def to_planes(idx, carry):
        g = idx // (tq // SUBLANES)
        r0 = pl.multiple_of((idx % (tq // SUBLANES)) * SUBLANES, SUBLANES)
        c0 = pl.multiple_of(g * group, group)
        words = [key_ref[pl.ds(r0, SUBLANES), pl.ds(c0 + jj * LANES, LANES)] for jj in range(nblk)] + pad
        planes = _bit_transpose32(words)
        planes[0] = ~planes[0]
        for b in range(WORD_BITS):
            plane_ref[b, pl.ds(r0, SUBLANES), pl.ds(pl.multiple_of(g * LANES, LANES), LANES)] = planes[b]
        return carry

    lax.fori_loop(0, n_groups * (tq // SUBLANES), to_planes, 0)
    w_all = alive_ref.shape[1]
    in_span = lax.broadcasted_iota(I32, (tq, w_all), 1) < n_groups * LANES
    alive_ref[...] = jnp.where(in_span, -1, 0).astype(I32)

    def bit_step(b, carry):
        k_rem, t_u = carry
        alive = alive_ref[...]
        ones = alive & plane_ref[b]
        cnt = _popcount_rows(ones)
        take = cnt >= k_rem
        alive_ref[...] = jnp.where(take, ones, alive ^ ones)
        return (jnp.where(take, k_rem, k_rem - cnt),
                jnp.where(take, t_u | lax.shift_left(jnp.int32(1), 31 - b), t_u))

    need, t_u = lax.fori_loop(0, WORD_BITS, bit_step,
                              (jnp.full((tq, 1), float(n_sel), F32), jnp.zeros((tq, 1), I32)))
    t = t_u ^ INT_MIN
    j = _resolve_ties(count, t, need, _popcount_rows(alive_ref[...]), tq, idx_bits)

    q = q_ref[...]
    lane_g = lax.broadcasted_iota(I32, (tq, KV_W), 1) // HEAD_DIM
    for r in range(DSA_GROUP):
        qr = q[:, r * KV_W:(r + 1) * KV_W]
        for g in range(DSA_KV_HEADS):
            hh = r * DSA_KV_HEADS + g
            qbd_ref[hh * tq:(hh + 1) * tq, :] = jnp.where(lane_g == g, qr, jnp.zeros_like(qr))
    _softmax_init(m_ref, l_ref, acc_ref)

    s_bufs = (s0_ref, s1_ref)

    def scores(c, slot):
        off = pl.multiple_of(c * cka, cka)
        s_bufs[slot][...] = jnp.dot(qbd_ref[...], kt_ref[:, pl.ds(off, cka)], preferred_element_type=F32)

    def consume(c, slot):
        off = pl.multiple_of(c * cka, cka)
        kc = key_ref[:, pl.ds(off, cka)]
        col = off + lax.broadcasted_iota(I32, (tq, cka), 1)
        sel = (kc > t) | ((kc == t) & (col < j))
        bias = jnp.where(sel, 0.0, NEG_BIG)
        s = s_bufs[slot][...] + jnp.concatenate([bias] * DSA_HEADS, axis=0)
        _softmax_step(s, v_ref[pl.ds(off, cka), :], m_ref, l_ref, acc_ref)

    _pipelined_chunks(n_att, scores, consume)
    o_full = acc_ref[...] / l_ref[...]
    for r in range(DSA_GROUP):
        o_r = jnp.zeros((tq, KV_W), F32)
        for g in range(DSA_KV_HEADS):
            hh = r * DSA_KV_HEADS + g
            o_r = o_r + jnp.where(lane_g == g, o_full[hh * tq:(hh + 1) * tq, :], 0.0)
        o_ref[:, r * KV_W:(r + 1) * KV_W] = o_r


def _resident(shape):
    nd = len(shape)
    return pl.BlockSpec(shape, lambda *_: (0,) * nd, pipeline_mode=pl.Buffered(1))


def _dsa_prompt(q16, iq16, iw, ikt16, kt16, v16, tq=128, ck=512, cka=1024):
    s_len = q16.shape[0]
    cka = min(cka, s_len)
    ck = min(ck, cka)
    n_sel = min(TOPK_MAX, s_len // 4)
    rows = DSA_HEADS * tq
    nblk = min(WORD_BITS, s_len // LANES)
    assert s_len % (nblk * LANES) == 0 and (nblk * LANES) % ck == 0
    n_words = s_len // nblk
    kern = functools.partial(_dsa_prompt_kernel, tq=tq, ck=ck, cka=cka, n_sel=n_sel,
                             idx_bits=int(s_len).bit_length(), nblk=nblk)
    return pl.pallas_call(
        kern,
        grid=(s_len // tq,),
        in_specs=[pl.BlockSpec((tq, DSA_WIDTH), lambda i: (i, 0)),
                  pl.BlockSpec((tq, IDX_HEADS * IDX_DIM), lambda i: (i, 0)),
                  pl.BlockSpec((tq, LANES), lambda i: (i, 0)),
                  _resident(ikt16.shape), _resident(kt16.shape), _resident(v16.shape)],
        out_specs=pl.BlockSpec((tq, DSA_WIDTH), lambda i: (i, 0)),
        out_shape=jax.ShapeDtypeStruct((s_len, DSA_WIDTH), F32),
        scratch_shapes=[pltpu.VMEM((tq, s_len), I32), pltpu.VMEM((WORD_BITS, tq, n_words), I32),
                        pltpu.VMEM((tq, n_words), I32), pltpu.VMEM((rows, KV_W), BF16),
                        pltpu.VMEM((rows, cka), F32), pltpu.VMEM((rows, cka), F32),
                        pltpu.VMEM((rows, 1), F32), pltpu.VMEM((rows, 1), F32), pltpu.VMEM((rows, KV_W), F32)],
        compiler_params=_cparams(("arbitrary",)),
        name="dsa_prompt",
    )(q16, iq16, iw, ikt16, kt16, v16)


def _softmax_step(s, vals, m_ref, l_ref, acc_ref, vals_keys_on_lanes=False):
    m_prev = m_ref[...]
    m_new = jnp.maximum(m_prev, jnp.max(s, axis=1, keepdims=True))
    alpha = jnp.exp(m_prev - m_new)
    p = jnp.exp(s - m_new)
    l_ref[...] = alpha * l_ref[...] + jnp.sum(p, axis=1, keepdims=True)
    if vals_keys_on_lanes:
        pv = lax.dot_general(p.astype(BF16), vals, _NT, preferred_element_type=F32)
    else:
        pv = jnp.dot(p.astype(BF16), vals, preferred_element_type=F32)
    acc_ref[...] = alpha * acc_ref[...] + pv
    m_ref[...] = m_new


def _softmax_init(m_ref, l_ref, acc_ref):
    m_ref[...] = jnp.full(m_ref.shape, NEG_BIG, F32)
    l_ref[...] = jnp.zeros(l_ref.shape, F32)
    acc_ref[...] = jnp.zeros(acc_ref.shape, F32)


def _mla_prompt_kernel(qcat_ref, kcat_ref, kcatt_ref, wuv_ref, o_ref, q_ref, s0_ref, s1_ref, m_ref, l_ref, acc_ref,
                       *, tq, ck):
    qb = pl.program_id(0)
    n_chunks = ((qb + 1) * tq + ck - 1) // ck
    row_pos = qb * tq + lax.broadcasted_iota(I32, (tq, 1), 0)
    for hh in range(MLA_HEADS):
        q_ref[hh * tq:(hh + 1) * tq, :] = qcat_ref[:, hh * QCAT_W:(hh + 1) * QCAT_W]
    _softmax_init(m_ref, l_ref, acc_ref)
    s_bufs = (s0_ref, s1_ref)

    def scores(c, slot):
        off = pl.multiple_of(c * ck, ck)
        s_bufs[slot][...] = jnp.dot(q_ref[...], kcatt_ref[:, pl.ds(off, ck)], preferred_element_type=F32)

    def consume(c, slot):
        off = pl.multiple_of(c * ck, ck)
        col = off + lax.broadcasted_iota(I32, (tq, ck), 1)
        bias = jnp.where(col <= row_pos, 0.0, NEG_BIG)
        s = s_bufs[slot][...] + jnp.concatenate([bias] * MLA_HEADS, axis=0)
        _softmax_step(s, kcat_ref[pl.ds(off, ck), :MLA_KV_LORA], m_ref, l_ref, acc_ref)

    _pipelined_chunks(n_chunks, scores, consume)
    o_lat = (acc_ref[...] / l_ref[...]).astype(BF16)
    lat = jnp.concatenate([o_lat[hh * tq:(hh + 1) * tq, :] for hh in range(MLA_HEADS)], axis=1)
    o_ref[...] = jnp.dot(lat, wuv_ref[...], preferred_element_type=F32)


def _mla_prompt(qcat, kcat, kcatt, wuv, tq=128, ck=1024):
    s_len = qcat.shape[0]
    ck = min(ck, s_len)
    rows = MLA_HEADS * tq
    return pl.pallas_call(
        functools.partial(_mla_prompt_kernel, tq=tq, ck=ck),
        grid=(s_len // tq,),
        in_specs=[pl.BlockSpec((tq, MLA_HEADS * QCAT_W), lambda i: (i, 0)),
                  _resident(kcat.shape), _resident(kcatt.shape), _resident(wuv.shape)],
        out_specs=pl.BlockSpec((tq, MLA_WIDTH), lambda i: (i, 0)),
        out_shape=jax.ShapeDtypeStruct((s_len, MLA_WIDTH), F32),
        scratch_shapes=[pltpu.VMEM((rows, QCAT_W), BF16), pltpu.VMEM((rows, ck), F32), pltpu.VMEM((rows, ck), F32),
                        pltpu.VMEM((rows, 1), F32), pltpu.VMEM((rows, 1), F32), pltpu.VMEM((rows, MLA_KV_LORA), F32)],
        compiler_params=_cparams(("arbitrary",)),
        name="mla_prompt",
    )(qcat, kcat, kcatt, wuv)


PAGES_PER_STEP = 64


def _pad_rows(x):
    return jnp.concatenate([x, jnp.zeros((PAGE_SIZE - x.shape[0], x.shape[1]), x.dtype)], axis=0)


def _tile_rows(x, n):
    return jnp.concatenate([x] * n, axis=0)


def _idx_scores(iq_rows, iw, keys_t16, t):
    r = jnp.dot(iq_rows, keys_t16, preferred_element_type=F32)
    sc = jnp.zeros((t, keys_t16.shape[1]), F32)
    for h in range(IDX_HEADS):
        sc = sc + iw[:, h:h + 1] * jnp.maximum(r[h * t:(h + 1) * t, :], 0.0)
    return sc


def _sample_idx_mla_kernel(pt_ref, iq_ref, iw_ref, ikn_ref, qcat_ref, ckvn_ref, krn_ref, wuv_ref, *rest,
                           pg, n_pages, n_sel, idx_bits, t):
    idx_refs, ckv_refs, kr_refs = rest[:pg], rest[pg:2 * pg], rest[2 * pg:3 * pg]
    o_ref, bias_ref = rest[3 * pg:3 * pg + 2]
    key_ref, iqr_ref, qrow_ref, m_ref, l_ref, acc_ref = rest[3 * pg + 2:]
    j = pl.program_id(1)
    past = n_pages * PAGE_SIZE
    span = pg * PAGE_SIZE

    @pl.when(j == 0)
    def _init():
        iq = iq_ref[...]
        for h in range(IDX_HEADS):
            iqr_ref[h * t:(h + 1) * t, :] = iq[:, h * IDX_DIM:(h + 1) * IDX_DIM]
        qc = qcat_ref[...]
        for hh in range(MLA_HEADS):
            qrow_ref[hh * t:(hh + 1) * t, :] = qc[:, hh * QCAT_W:(hh + 1) * QCAT_W]
        _softmax_init(m_ref, l_ref, acc_ref)

    iqr = iqr_ref[...].astype(BF16)
    iw = iw_ref[...]
    qrow = qrow_ref[...]
    q_lat = qrow[:, :MLA_KV_LORA].astype(BF16)
    q_rope = qrow[:, MLA_KV_LORA + IDX_DIM:MLA_KV_LORA + IDX_DIM + MLA_ROPE].astype(BF16)

    def mla_scores(ck16, kr_t16):
        return (lax.dot_general(q_lat, ck16, _NT, preferred_element_type=F32)
                + jnp.dot(q_rope, kr_t16, preferred_element_type=F32))

    ikc = jnp.concatenate([r[...] for r in idx_refs], axis=1).astype(BF16)
    ckc = jnp.concatenate([r[...] for r in ckv_refs], axis=0).astype(BF16)
    krc = jnp.concatenate([r[...] for r in kr_refs], axis=1).astype(BF16)
    key_ref[:, pl.ds(pl.multiple_of(j * span, span), span)] = _sortable(_idx_scores(iqr, iw, ikc, t))
    _softmax_step(mla_scores(ckc, krc), ckc, m_ref, l_ref, acc_ref)

    @pl.when(j == n_pages // pg - 1)
    def _finish():
        tok = lax.broadcasted_iota(I32, (t, PAGE_SIZE), 0)
        col = lax.broadcasted_iota(I32, (t, PAGE_SIZE), 1)
        vis = col <= tok
        ikn = _pad_rows(ikn_ref[...]).T.astype(BF16)
        key_ref[:, past:past + PAGE_SIZE] = jnp.where(vis, _sortable(_idx_scores(iqr, iw, ikn, t)), INT_MIN)
        ckn = _pad_rows(ckvn_ref[...]).astype(BF16)
        krn = _pad_rows(krn_ref[...]).T.astype(BF16)
        s = mla_scores(ckn, krn) + _tile_rows(jnp.where(vis, 0.0, NEG_BIG), MLA_HEADS)
        _softmax_step(s, ckn, m_ref, l_ref, acc_ref)
        o_lat = acc_ref[...] / l_ref[...]
        lat = jnp.concatenate([o_lat[hh * t:(hh + 1) * t, :] for hh in range(MLA_HEADS)], axis=1)
        o_ref[...] = jnp.dot(lat.astype(BF16), wuv_ref[...], preferred_element_type=F32)

        keys = key_ref[...]
        cols = lax.broadcasted_iota(I32, keys.shape, 1)

        def count(pred):
            return jnp.sum(jnp.where(pred(keys, cols), 1.0, 0.0), axis=1, keepdims=True)

        thr, jj = _select_threshold(count, n_sel, t, idx_bits)
        sel = (keys > thr) | ((keys == thr) & (cols < jj))
        bias_ref[...] = jnp.where(sel, 0.0, NEG_BIG)


def _page_specs(page_shape, pg, n_pages, page_off):
    return [pl.BlockSpec((None,) + page_shape,
                         functools.partial(lambda s, j, pt, i: (page_off + pt[s * n_pages + j * pg + i], 0, 0), i=i))
            for i in range(pg)]


def _key_major(cache, depth_pool):
    return jnp.moveaxis(cache, 2, -1).reshape(depth_pool, -1, PAGE_SIZE)


def _seq_spec(shape):
    return pl.BlockSpec((None,) + tuple(shape[1:]), lambda s, j, pt: (s, 0, 0))


def _sample_idx_mla(pt_flat, iq, iw, ikn, qcat, ckvn, krn, wuv, c_idx, c_ckv, c_kr, n_pages, page_off=0):
    nseq, t, _ = iq.shape
    pg = min(PAGES_PER_STEP, n_pages)
    past = n_pages * PAGE_SIZE
    lp = past + PAGE_SIZE
    n_sel = min(TOPK_MAX, (past + t) // 4)
    rows = MLA_HEADS * t
    kern = functools.partial(_sample_idx_mla_kernel, pg=pg, n_pages=n_pages, n_sel=n_sel,
                             idx_bits=int(past + t).bit_length(), t=t)
    grid_spec = pltpu.PrefetchScalarGridSpec(
        num_scalar_prefetch=1,
        grid=(nseq, n_pages // pg),
        in_specs=[_seq_spec(iq.shape), _seq_spec(iw.shape), _seq_spec(ikn.shape), _seq_spec(qcat.shape),
                  _seq_spec(ckvn.shape), _seq_spec(krn.shape),
                  pl.BlockSpec(wuv.shape, lambda s, j, pt: (0, 0))]
                 + _page_specs((IDX_DIM, PAGE_SIZE), pg, n_pages, page_off)
                 + _page_specs((PAGE_SIZE, MLA_KV_LORA), pg, n_pages, page_off)
                 + _page_specs((MLA_ROPE, PAGE_SIZE), pg, n_pages, page_off),
        out_specs=[pl.BlockSpec((None, t, MLA_WIDTH), lambda s, j, pt: (s, 0, 0)),
                   pl.BlockSpec((None, t, lp), lambda s, j, pt: (s, 0, 0))],
        scratch_shapes=[pltpu.VMEM((t, lp), I32), pltpu.VMEM((IDX_HEADS * t, IDX_DIM), F32),
                        pltpu.VMEM((rows, QCAT_W), F32), pltpu.VMEM((rows, 1), F32),
                        pltpu.VMEM((rows, 1), F32), pltpu.VMEM((rows, MLA_KV_LORA), F32)],
    )
    return pl.pallas_call(
        kern, grid_spec=grid_spec,
        out_shape=[jax.ShapeDtypeStruct((nseq, t, MLA_WIDTH), F32), jax.ShapeDtypeStruct((nseq, t, lp), F32)],
        compiler_params=_cparams(("arbitrary", "arbitrary")),
        name="sample_idx_mla",
    )(pt_flat, iq, iw, ikn, qcat, ckvn, krn, wuv, *([c_idx] * pg), *([c_ckv] * pg), *([c_kr] * pg))


def _sample_dsa_kernel(pt_ref, q_ref, kn_ref, vn_ref, bias_ref, biasn_ref, *rest, pg, n_pages, t):
    k_refs, v_refs = rest[:pg], rest[pg:2 * pg]
    o_ref = rest[2 * pg]
    qbd_ref, m_ref, l_ref, acc_ref = rest[2 * pg + 1:]
    j = pl.program_id(1)
    lane_g = lax.broadcasted_iota(I32, (t, KV_W), 1) // HEAD_DIM

    @pl.when(j == 0)
    def _init():
        q = q_ref[...]
        for r in range(DSA_GROUP):
            qr = q[:, r * KV_W:(r + 1) * KV_W]
            for g in range(DSA_KV_HEADS):
                hh = r * DSA_KV_HEADS + g
                qbd_ref[hh * t:(hh + 1) * t, :] = jnp.where(lane_g == g, qr, 0.0)
        _softmax_init(m_ref, l_ref, acc_ref)

    qbd = qbd_ref[...].astype(BF16)
    kc = jnp.concatenate([r[...] for r in k_refs], axis=1).astype(BF16)
    vc = jnp.concatenate([r[...] for r in v_refs], axis=1).astype(BF16)
    s = jnp.dot(qbd, kc, preferred_element_type=F32) + _tile_rows(bias_ref[...], DSA_HEADS)
    _softmax_step(s, vc, m_ref, l_ref, acc_ref, vals_keys_on_lanes=True)

    @pl.when(j == n_pages // pg - 1)
    def _finish():
        kn = _pad_rows(kn_ref[...]).astype(BF16)
        vn = _pad_rows(vn_ref[...]).astype(BF16)
        s = lax.dot_general(qbd, kn, _NT, preferred_element_type=F32) + _tile_rows(biasn_ref[...], DSA_HEADS)
        _softmax_step(s, vn, m_ref, l_ref, acc_ref)
        o_full = acc_ref[...] / l_ref[...]
        for r in range(DSA_GROUP):
            o_r = jnp.zeros((t, KV_W), F32)
            for g in range(DSA_KV_HEADS):
                hh = r * DSA_KV_HEADS + g
                o_r = o_r + jnp.where(lane_g == g, o_full[hh * t:(hh + 1) * t, :], 0.0)
            o_ref[:, r * KV_W:(r + 1) * KV_W] = o_r


def _sample_dsa(pt_flat, q, kn, vn, bias, c_k, c_v, n_pages, page_off=0):
    nseq, t, _ = q.shape
    pg = min(PAGES_PER_STEP, n_pages)
    rows = DSA_HEADS * t
    span = pg * PAGE_SIZE
    grid_spec = pltpu.PrefetchScalarGridSpec(
        num_scalar_prefetch=1,
        grid=(nseq, n_pages // pg),
        in_specs=[_seq_spec(q.shape), _seq_spec(kn.shape), _seq_spec(vn.shape),
                  pl.BlockSpec((None, t, span), lambda s, j, pt: (s, 0, j)),
                  pl.BlockSpec((None, t, PAGE_SIZE), lambda s, j, pt: (s, 0, n_pages))]
                 + _page_specs((KV_W, PAGE_SIZE), pg, n_pages, page_off)
                 + _page_specs((KV_W, PAGE_SIZE), pg, n_pages, page_off),
        out_specs=pl.BlockSpec((None, t, DSA_WIDTH), lambda s, j, pt: (s, 0, 0)),
        scratch_shapes=[pltpu.VMEM((rows, KV_W), F32), pltpu.VMEM((rows, 1), F32),
                        pltpu.VMEM((rows, 1), F32), pltpu.VMEM((rows, KV_W), F32)],
    )
    return pl.pallas_call(
        functools.partial(_sample_dsa_kernel, pg=pg, n_pages=n_pages, t=t), grid_spec=grid_spec,
        out_shape=jax.ShapeDtypeStruct((nseq, t, DSA_WIDTH), F32),
        compiler_params=_cparams(("arbitrary", "arbitrary")),
        name="sample_dsa",
    )(pt_flat, q, kn, vn, bias, bias, *([c_k] * pg), *([c_v] * pg))


def _merge_kernel(x_ref, od_ref, om_ref, gate_ref, gd_ref, gm_ref, wo_ref, o_ref):
    a = jnp.concatenate([_rms(od_ref[...]) * gd_ref[...], _rms(om_ref[...]) * gm_ref[...]], axis=1)
    y = jnp.dot(a.astype(BF16), wo_ref[...], preferred_element_type=F32)
    o_ref[...] = x_ref[...] + gate_ref[...] * y


def _mod_spec(a, tm):
    if a.shape[0] == 1:
        return pl.BlockSpec((1, a.shape[1]), lambda i, *_: (0, 0))
    return pl.BlockSpec((tm, a.shape[1]), lambda i, *_: (i, 0))


def _merge(x, o_dsa, o_mla, gate, g_dsa, g_mla, w_out16, tm=256):
    t = x.shape[0]
    tm = min(tm, t)

    def row(w):
        return pl.BlockSpec((tm, w), lambda i: (i, 0))

    return pl.pallas_call(
        _merge_kernel,
        grid=(t // tm,),
        in_specs=[row(D_MODEL), row(DSA_WIDTH), row(MLA_WIDTH), _mod_spec(gate, tm),
                  _full_spec((1, DSA_WIDTH)), _full_spec((1, MLA_WIDTH)), _full_spec(w_out16.shape)],
        out_specs=row(D_MODEL),
        out_shape=jax.ShapeDtypeStruct((t, D_MODEL), F32),
        compiler_params=_cparams(("arbitrary",)),
        name="merge",
    )(x, o_dsa, o_mla, gate, g_dsa.reshape(1, -1), g_mla.reshape(1, -1), w_out16)


EXPERT_CHUNK = 2048
I_PER_CHUNK = EXPERT_CHUNK // PEER_NKEYS


def _erf(x):
    return lax.erf(x)


def _extract_top(cur, n):
    vals = []
    for _ in range(n):
        mx = jnp.max(cur, axis=0, keepdims=True)
        vals.append(mx)
        cur = jnp.where(cur == mx, -jnp.inf, cur)
    return vals


def _peer_kernel(x1_ref, shift_ref, scale_ref, gate_ref, gffn_ref, gfin_ref, wpq_ref, keys_ref, u_ref, vt_ref,
                 y_ref, ht_ref, qry_ref, thr_ref, s1_ref, e0_ref, e1_ref, act_ref, ga_ref, acc_ref,
                 *, tm, final_norm):
    c = pl.program_id(1)

    @pl.when(c == 0)
    def _prep():
        h = _rms(x1_ref[...]) * gffn_ref[...] * (1.0 + scale_ref[...]) + shift_ref[...]
        ht_ref[...] = h.T.astype(BF16)
        qry_ref[...] = jnp.dot(wpq_ref[...], ht_ref[...], preferred_element_type=F32)
        acc_ref[...] = jnp.zeros(acc_ref.shape, F32)

        def head(hh, carry):
            subs, tops = [], []
            for p in range(2):
                blk = pl.multiple_of((hh * 2 + p) * PEER_HALF, PEER_HALF)
                sub = jnp.dot(keys_ref[pl.ds(blk, PEER_NKEYS), :], qry_ref[pl.ds(blk, PEER_HALF), :],
                              preferred_element_type=F32, precision=lax.Precision.HIGHEST)
                subs.append(sub)
                tops.append(_extract_top(sub, PEER_TOPK))
            (s0, s1), (top0, top1) = subs, tops
            t1 = jnp.concatenate(top1, axis=0)
            slabs = [top0[a] + t1 for a in range(PEER_TOPK)]
            cand = jnp.concatenate(slabs, axis=0)
            tau = _extract_top(cand, PEER_TOPK)[-1]
            cmax = top0[0] + top1[0]
            z = jnp.sum(jnp.where(cand >= tau, jnp.exp(cand - cmax), 0.0), axis=0, keepdims=True)
            thr = jnp.full((PEER_NKEYS, tm), jnp.inf, F32)
            for a in range(PEER_TOPK):
                thr_a = jnp.min(jnp.where(slabs[a] >= tau, t1, jnp.inf), axis=0, keepdims=True)
                thr = jnp.where(s0 == top0[a], thr_a, thr)
            thr_ref[hh] = thr
            s1_ref[hh] = s1
            e0_ref[hh] = jnp.exp(s0 - top0[0]) / z
            e1_ref[hh] = jnp.exp(s1 - top1[0])
            return carry

        lax.fori_loop(0, PEER_HEADS, head, 0)

    a = jnp.dot(u_ref[...], ht_ref[...], preferred_element_type=F32)
    act_ref[...] = 0.5 * a * (1.0 + _erf(a * (2.0 ** -0.5)))

    for il in range(I_PER_CHUNK):
        i = c * I_PER_CHUNK + il
        g = jnp.zeros((PEER_NKEYS, tm), F32)
        for hh in range(PEER_HEADS):
            hit = s1_ref[hh] >= thr_ref[hh, pl.ds(i, 1), :]
            g = g + jnp.where(hit, e1_ref[hh], 0.0) * e0_ref[hh, pl.ds(i, 1), :]
        rows = slice(il * PEER_NKEYS, (il + 1) * PEER_NKEYS)
        ga_ref[rows, :] = (g * act_ref[rows, :]).astype(BF16)
    acc_ref[...] += jnp.dot(vt_ref[...], ga_ref[...], preferred_element_type=F32)

    @pl.when(c == pl.num_programs(1) - 1)
    def _finish():
        x2 = x1_ref[...] + gate_ref[...] * acc_ref[...].T
        y_ref[...] = _rms(x2) * gfin_ref[...] if final_norm else x2


def _peer_final(x1, shift, scale, gate, g_ffn, g_final, wpq_t16, keys2d, u16, vt16, final_norm, tm=256):
    t = x1.shape[0]
    tm = min(tm, t)
    n_chunks = N_EXPERTS // EXPERT_CHUNK
    hshape = (PEER_HEADS, PEER_NKEYS, tm)
    return pl.pallas_call(
        functools.partial(_peer_kernel, tm=tm, final_norm=final_norm),
        grid=(t // tm, n_chunks),
        in_specs=[pl.BlockSpec((tm, D_MODEL), lambda i, c: (i, 0)),
                  _mod_spec(shift, tm), _mod_spec(scale, tm), _mod_spec(gate, tm),
                  _full_spec((1, D_MODEL)), _full_spec((1, D_MODEL)),
                  _resident(wpq_t16.shape), _resident(keys2d.shape),
                  pl.BlockSpec((EXPERT_CHUNK, D_MODEL), lambda i, c: (c, 0)),
                  pl.BlockSpec((None, D_MODEL, EXPERT_CHUNK), lambda i, c: (c, 0, 0))],
        out_specs=pl.BlockSpec((tm, D_MODEL), lambda i, c: (i, 0)),
        out_shape=jax.ShapeDtypeStruct((t, D_MODEL), F32),
        scratch_shapes=[pltpu.VMEM((D_MODEL, tm), BF16), pltpu.VMEM((2 * PEER_HEADS * PEER_HALF, tm), F32),
                        pltpu.VMEM(hshape, F32), pltpu.VMEM(hshape, F32), pltpu.VMEM(hshape, F32),
                        pltpu.VMEM(hshape, F32),
                        pltpu.VMEM((EXPERT_CHUNK, tm), F32), pltpu.VMEM((EXPERT_CHUNK, tm), BF16),
                        pltpu.VMEM((D_MODEL, tm), F32)],
        compiler_params=_cparams(("arbitrary", "arbitrary")),
        name="peer_final",
    )(x1, shift, scale, gate, g_ffn.reshape(1, -1), g_final.reshape(1, -1), wpq_t16, keys2d, u16, vt16)


def _mod_rows(m, k, n_rep):
    rows = m[:, k * D_MODEL:(k + 1) * D_MODEL]
    return rows if rows.shape[0] == 1 else jnp.repeat(rows, n_rep, axis=0)


def _perm_heads(a):
    tail = a.shape[1:]
    return a.reshape((DSA_KV_HEADS, DSA_GROUP, HEAD_DIM) + tail).swapaxes(0, 1).reshape((DSA_WIDTH,) + tail)


def kernel(x_prompt, x_sample, cache_dsa_k, cache_dsa_v, cache_idx_k, cache_mla_ckv, cache_mla_krope, page_table, c_prompt, c_sample, w_mod, b_mod, g_mix, g_ffn, w_in, g_cq, g_ckv, w_uq, w_uk, w_uv, g_out_dsa, g_out_mla, w_out, w_pq, peer_keys, peer_u, peer_v, g_final):
    depth = w_mod.shape[0]
    b, s_len, _ = x_prompt.shape
    nseq, t, _ = x_sample.shape
    n_pages = page_table.shape[1]
    past = n_pages * PAGE_SIZE
    n_pool = cache_dsa_k.shape[1]
    xp = x_prompt.reshape(b * s_len, D_MODEL)
    xs = x_sample.reshape(nseq * t, D_MODEL)
    cos_p, sin_p = _rope_tables(jnp.tile(jnp.arange(s_len), b))
    cos_s, sin_s = _rope_tables(jnp.tile(past + jnp.arange(t), nseq))
    pt_flat = page_table.reshape(-1)
    c_k = _key_major(cache_dsa_k, depth * n_pool)
    c_v = _key_major(cache_dsa_v, depth * n_pool)
    c_idx = _key_major(cache_idx_k, depth * n_pool)
    c_kr = _key_major(cache_mla_krope, depth * n_pool)
    c_ckv = cache_mla_ckv.reshape(depth * n_pool, PAGE_SIZE, MLA_KV_LORA)
    new_p = [[] for _ in range(5)]
    new_s = [[] for _ in range(5)]

    for l in range(depth):
        last = l == depth - 1
        mod = _adaln(jnp.concatenate([c_prompt, c_sample], axis=0), w_mod[l], b_mod[l])
        mp, ms = mod[:b], mod[b:]
        wqn, wqr, wqrr, wuk, wuv = _pack_mla_weights(w_uq[l], w_uk[l], w_uv[l])
        packed = _pack_in_weights(w_in[l]) + (wqn, wqr, wqrr, wuk)
        g_dsa = _perm_heads(g_out_dsa[l])
        w_out16 = jnp.concatenate([_perm_heads(w_out[l][:DSA_WIDTH]), w_out[l][DSA_WIDTH:]], axis=0).astype(BF16)
        vt16 = peer_v[l].reshape(N_EXPERTS // EXPERT_CHUNK, EXPERT_CHUNK, D_MODEL).swapaxes(1, 2).astype(BF16)
        peer_w = (w_pq[l].T.astype(BF16), peer_keys[l].reshape(2 * PEER_HEADS * PEER_NKEYS, PEER_HALF),
                  peer_u[l].astype(BF16), vt16)

        (k32, v32, ik32, ckv32, kr32, q16, iq16, iw, v16, kcat, qcat, kt16, ikt16, kcatt) = _project(
            xp, _mod_rows(mp, 0, s_len), _mod_rows(mp, 1, s_len), g_mix[l], packed, cos_p, sin_p, g_cq[l], g_ckv[l])
        o_dsa, o_mla = [], []
        for bi in range(b):
            sl = slice(bi * s_len, (bi + 1) * s_len)
            o_dsa.append(_dsa_prompt(q16[sl], iq16[sl], iw[sl], ikt16[:, sl], kt16[:, sl], v16[sl]))
            o_mla.append(_mla_prompt(qcat[sl], kcat[sl], kcatt[:, sl], wuv))
        o_dsa = o_dsa[0] if b == 1 else jnp.concatenate(o_dsa, axis=0)
        o_mla = o_mla[0] if b == 1 else jnp.concatenate(o_mla, axis=0)
        x1 = _merge(xp, o_dsa, o_mla, _mod_rows(mp, 2, s_len), g_dsa, g_out_mla[l], w_out16)
        xp = _peer_final(x1, _mod_rows(mp, 3, s_len), _mod_rows(mp, 4, s_len), _mod_rows(mp, 5, s_len),
                         g_ffn[l], g_final, *peer_w, final_norm=last)
        for lst, (val, shp) in zip(new_p, ((k32, (DSA_KV_HEADS, HEAD_DIM)), (v32, (DSA_KV_HEADS, HEAD_DIM)),
                                           (ik32, (IDX_DIM,)), (ckv32, (MLA_KV_LORA,)), (kr32, (MLA_ROPE,)))):
            lst.append(val.reshape((b, s_len) + shp))

        ms_tok = [jnp.repeat(ms[:, k * D_MODEL:(k + 1) * D_MODEL], t, axis=0) for k in range(6)]
        (k32, v32, ik32, ckv32, kr32, q16, iq16, iw, v16, kcat, qcat, kt16, ikt16, kcatt) = _project(
            xs, ms_tok[0], ms_tok[1], g_mix[l], packed, cos_s, sin_s, g_cq[l], g_ckv[l], tm=min(256, nseq * t))

        def seq3(a):
            return a.reshape(nseq, t, a.shape[-1]).astype(F32)

        o_mla, bias = _sample_idx_mla(pt_flat, seq3(iq16), seq3(iw), seq3(ik32), seq3(qcat), seq3(ckv32),
                                      seq3(kr32), wuv, c_idx, c_ckv, c_kr, n_pages, page_off=l * n_pool)
        o_dsa = _sample_dsa(pt_flat, seq3(q16), seq3(k32), seq3(v32), bias, c_k, c_v, n_pages,
                            page_off=l * n_pool)
        x1 = _merge(xs, o_dsa.reshape(nseq * t, DSA_WIDTH), o_mla.reshape(nseq * t, MLA_WIDTH), ms_tok[2],
                    g_dsa, g_out_mla[l], w_out16)
        xs = _peer_final(x1, ms_tok[3], ms_tok[4], ms_tok[5], g_ffn[l], g_final, *peer_w, final_norm=last)
        for lst, (val, shp) in zip(new_s, ((k32, (DSA_KV_HEADS, HEAD_DIM)), (v32, (DSA_KV_HEADS, HEAD_DIM)),
                                           (ik32, (IDX_DIM,)), (ckv32, (MLA_KV_LORA,)), (kr32, (MLA_ROPE,)))):
            lst.append(val.reshape((nseq, t) + shp))

    sp = [jnp.stack(a, axis=0) for a in new_p]
    ss = [jnp.stack(a, axis=0) for a in new_s]
    return (xp.reshape(b, s_len, D_MODEL), xs.reshape(nseq, t, D_MODEL),
            sp[0], sp[1], sp[2], sp[3], sp[4], ss[0], ss[1], ss[2], ss[3], ss[4])
```

```python
import functools

import numpy as np
import jax
import jax.numpy as jnp
from jax import lax
from jax.experimental import pallas as pl
from jax.experimental.pallas import tpu as pltpu

F32 = jnp.float32
BF16 = jnp.bfloat16
I32 = jnp.int32

D_MODEL = 1024
HEAD_DIM = 64
DSA_WIDTH = 512
DSA_HEADS = 8
DSA_KV_HEADS = 4
DSA_GROUP = 2
KV_W = DSA_KV_HEADS * HEAD_DIM
IDX_HEADS = 4
IDX_DIM = 64
TOPK_MAX = 256
MLA_V = 64
MLA_WIDTH = 512
MLA_HEADS = 8
MLA_NOPE = 64
MLA_ROPE = 32
MLA_Q_LORA = 384
MLA_KV_LORA = 256
MLA_SCALE = (MLA_NOPE + MLA_ROPE) ** -0.5
PEER_HEADS = 8
PEER_NKEYS = 128
N_EXPERTS = PEER_NKEYS * PEER_NKEYS
PEER_HALF = 128
PEER_TOPK = 16
PAGE_SIZE = 128
ROPE_THETA = 10000.0
EPS = 1e-6
IN_SPLITS = (DSA_WIDTH, KV_W, KV_W, IDX_HEADS * IDX_DIM, IDX_HEADS, IDX_DIM,
             MLA_Q_LORA, MLA_KV_LORA, MLA_ROPE)

LANES = 128
SUBLANES = 8
VMEM_LIMIT = 56 * 1024 * 1024

C_Q, C_K, C_IQ, C_D, C_V, C_CKV, C_CQ, C_IW, C_END = 0, 512, 768, 1024, 1152, 1408, 1664, 2048, 2176
ROT_W = C_V
QCAT_W = 384
INT_MIN = -2147483648
INT_MAX = 2147483647
NEG_BIG = -1e30


def _cparams(sem):
    return pltpu.CompilerParams(dimension_semantics=sem, vmem_limit_bytes=VMEM_LIMIT)


def _full_spec(shape):
    nd = len(shape)
    return pl.BlockSpec(shape, lambda *_: (0,) * nd)


def _adaln_kernel(c_ref, w_ref, b_ref, o_ref):
    c = c_ref[...]
    a = c / (1.0 + jnp.exp(-c))
    o_ref[...] = jnp.dot(a, w_ref[...], preferred_element_type=F32,
                         precision=lax.Precision.HIGHEST) + b_ref[...]


def _adaln(c, w_mod, b_mod):
    n = c.shape[0]
    n_pad = -(-n // SUBLANES) * SUBLANES
    cp = jnp.pad(c, ((0, n_pad - n), (0, 0)))
    tn = 1536
    out = pl.pallas_call(
        _adaln_kernel,
        grid=(w_mod.shape[1] // tn,),
        in_specs=[pl.BlockSpec((n_pad, D_MODEL), lambda j: (0, 0)),
                  pl.BlockSpec((D_MODEL, tn), lambda j: (0, j)),
                  pl.BlockSpec((1, tn), lambda j: (0, j))],
        out_specs=pl.BlockSpec((n_pad, tn), lambda j: (0, j)),
        out_shape=jax.ShapeDtypeStruct((n_pad, w_mod.shape[1]), F32),
        compiler_params=_cparams(("arbitrary",)),
        name="adaln",
    )(cp, w_mod, b_mod.reshape(1, -1))
    return out[:n]


def _swap_halves(w, hd):
    d, n = w.shape
    w4 = w.reshape(d, n // hd, 2, hd // 2)
    return w4[:, :, ::-1, :].reshape(d, n)


def _pack_in_weights(w_in):
    cuts = [int(v) for v in np.cumsum(IN_SPLITS)[:-1]]
    wq, wk, wv, wiq, wiw, wik, wcq, wckv, wkr = jnp.split(w_in, cuts, axis=1)
    wq = wq.reshape(D_MODEL, DSA_KV_HEADS, DSA_GROUP, HEAD_DIM).transpose(0, 2, 1, 3).reshape(D_MODEL, DSA_WIDTH)
    pad32 = jnp.zeros((D_MODEL, 32), F32)
    w_d = jnp.concatenate([wik, wkr, pad32], axis=1)
    w_d_rot = jnp.concatenate([_swap_halves(wik, IDX_DIM), _swap_halves(wkr, MLA_ROPE), pad32], axis=1)
    wiw_p = jnp.pad(wiw, ((0, 0), (0, LANES - IDX_HEADS)))
    w_main = jnp.concatenate([wq, wk, wiq, w_d, wv, wckv, wcq, wiw_p], axis=1)
    w_rot = jnp.concatenate([_swap_halves(wq, HEAD_DIM), _swap_halves(wk, HEAD_DIM),
                             _swap_halves(wiq, IDX_DIM), w_d_rot], axis=1)
    return w_main.astype(BF16), w_rot.astype(BF16)


def _pack_mla_weights(w_uq, w_uk, w_uv):
    w_n = w_uq[:, :, :MLA_NOPE].reshape(MLA_Q_LORA, MLA_HEADS * MLA_NOPE)
    w_r = w_uq[:, :, MLA_NOPE:]
    w_r_rot = jnp.concatenate([w_r[..., MLA_ROPE // 2:], w_r[..., :MLA_ROPE // 2]], axis=-1)

    def place(w):
        z = jnp.zeros((MLA_Q_LORA, MLA_HEADS, LANES), F32)
        return z.at[:, :, IDX_DIM:IDX_DIM + MLA_ROPE].set(w).reshape(MLA_Q_LORA, MLA_HEADS * LANES)

    eye = jnp.eye(MLA_HEADS, dtype=F32)
    uk = jnp.einsum('chd,hk->hdkc', w_uk, eye).reshape(MLA_HEADS * MLA_NOPE, MLA_HEADS * MLA_KV_LORA)
    uv = jnp.einsum('chd,hk->hckd', w_uv, eye).reshape(MLA_HEADS * MLA_KV_LORA, MLA_HEADS * MLA_V)
    return (w_n.astype(BF16), place(w_r).astype(BF16), place(w_r_rot).astype(BF16),
            uk.astype(BF16), uv.astype(BF16))


def _rope_tables(pos):
    pos = np.asarray(pos, np.float32)[:, None]

    def cs(dim):
        half = dim // 2
        inv = np.float32(ROPE_THETA) ** (-np.arange(half, dtype=np.float32) / np.float32(half))
        ang = (pos * inv[None, :]).astype(np.float32)
        c, s = np.cos(ang), np.sin(ang)
        return np.concatenate([c, c], axis=1), np.concatenate([-s, s], axis=1)

    c64, s64 = cs(HEAD_DIM)
    c32, s32 = cs(MLA_ROPE)
    z = np.zeros((pos.shape[0], 32), np.float32)
    return (jnp.asarray(np.concatenate([c64, c64, c64, c32, z], axis=1), F32),
            jnp.asarray(np.concatenate([s64, s64, s64, s32, z], axis=1), F32))


def _rms(x):
    return x * lax.rsqrt(jnp.mean(x * x, axis=-1, keepdims=True) + EPS)


def _proj_kernel(x_ref, shift_ref, scale_ref, gmix_ref, wm_ref, wr_ref, cos_ref, sin_ref,
                 gcq_ref, gckv_ref, wqn_ref, wqr_ref, wqrr_ref, wuk_ref,
                 k32_ref, v32_ref, ik32_ref, ckv32_ref, kr32_ref,
                 q16_ref, iq16_ref, iw_ref, v16_ref, kcat_ref, qcat_ref, kt16_ref, ikt16_ref, kcatt_ref):
    x = x_ref[...]
    h = _rms(x) * gmix_ref[...] * (1.0 + scale_ref[...]) + shift_ref[...]
    hb = h.astype(BF16)
    main = jnp.dot(hb, wm_ref[...], preferred_element_type=F32)
    rot = jnp.dot(hb, wr_ref[...], preferred_element_type=F32)
    cos = cos_ref[...]
    sin = sin_ref[...]
    cos_h, sin_h = cos[:, :LANES], sin[:, :LANES]
    cos_d, sin_d = cos[:, LANES:], sin[:, LANES:]

    def rope(lo, hi):
        reps = (hi - lo) // LANES
        c = jnp.concatenate([cos_h] * reps, axis=1)
        s = jnp.concatenate([sin_h] * reps, axis=1)
        return main[:, lo:hi] * c + rot[:, lo:hi] * s

    q = rope(C_Q, C_K)
    k = rope(C_K, C_IQ)
    iq = rope(C_IQ, C_D)
    sec_d = main[:, C_D:C_V] * cos_d + rot[:, C_D:C_V] * sin_d
    v = main[:, C_V:C_CKV]
    ckv = _rms(main[:, C_CKV:C_CQ]) * gckv_ref[...]
    cqn = (_rms(main[:, C_CQ:C_IW]) * gcq_ref[...]).astype(BF16)

    k32_ref[...] = k
    v32_ref[...] = v
    ik32_ref[...] = sec_d[:, :IDX_DIM]
    ckv32_ref[...] = ckv
    kr32_ref[...] = sec_d[:, IDX_DIM:IDX_DIM + MLA_ROPE]
    q16_ref[...] = (q * HEAD_DIM ** -0.5).astype(BF16)
    iq16_ref[...] = iq.astype(BF16)
    iw_ref[...] = main[:, C_IW:C_END] * (IDX_HEADS ** -0.5 * IDX_DIM ** -0.5)
    v16_ref[...] = v.astype(BF16)
    lane = lax.broadcasted_iota(I32, sec_d.shape, 1)
    kr_blk = jnp.where((lane >= IDX_DIM) & (lane < IDX_DIM + MLA_ROPE), sec_d, 0.0)
    kcat_ref[...] = jnp.concatenate([ckv, kr_blk], axis=1).astype(BF16)
    kt16_ref[...] = k.T.astype(BF16)
    ikt16_ref[...] = sec_d.T[:IDX_DIM, :].astype(BF16)
    kcatt_ref[...] = jnp.concatenate([ckv.T, kr_blk.T], axis=0).astype(BF16)

    qn = jnp.dot(cqn, wqn_ref[...], preferred_element_type=F32).astype(BF16)
    qlat = jnp.dot(qn, wuk_ref[...], preferred_element_type=F32) * MLA_SCALE
    qr_m = jnp.dot(cqn, wqr_ref[...], preferred_element_type=F32)
    qr_r = jnp.dot(cqn, wqrr_ref[...], preferred_element_type=F32)
    cd = jnp.concatenate([cos_d] * MLA_HEADS, axis=1)
    sd = jnp.concatenate([sin_d] * MLA_HEADS, axis=1)
    qr = (qr_m * cd + qr_r * sd) * MLA_SCALE
    for hh in range(MLA_HEADS):
        qcat_ref[:, hh * QCAT_W:hh * QCAT_W + MLA_KV_LORA] = (
            qlat[:, hh * MLA_KV_LORA:(hh + 1) * MLA_KV_LORA].astype(BF16))
        qcat_ref[:, hh * QCAT_W + MLA_KV_LORA:(hh + 1) * QCAT_W] = (
            qr[:, hh * LANES:(hh + 1) * LANES].astype(BF16))


def _project(x, shift, scale, g_mix, packed, cos_t, sin_t, g_cq, g_ckv, tm=256):
    t = x.shape[0]
    w_main, w_rot, wqn, wqr, wqrr, wuk = packed
    per_tok = shift.shape[0] != 1
    mod_spec = pl.BlockSpec((tm, D_MODEL), lambda i: (i, 0)) if per_tok else _full_spec((1, D_MODEL))

    def row(w):
        return pl.BlockSpec((tm, w), lambda i: (i, 0))

    widths = [(KV_W, F32), (KV_W, F32), (IDX_DIM, F32), (MLA_KV_LORA, F32), (MLA_ROPE, F32),
              (DSA_WIDTH, BF16), (IDX_HEADS * IDX_DIM, BF16), (LANES, F32), (KV_W, BF16),
              (QCAT_W, BF16), (MLA_HEADS * QCAT_W, BF16)]
    heights = [KV_W, IDX_DIM, QCAT_W]
    return pl.pallas_call(
        _proj_kernel,
        grid=(t // tm,),
        in_specs=[row(D_MODEL), mod_spec, mod_spec, _full_spec((1, D_MODEL)),
                  _full_spec(w_main.shape), _full_spec(w_rot.shape), row(2 * LANES), row(2 * LANES),
                  _full_spec((1, MLA_Q_LORA)), _full_spec((1, MLA_KV_LORA)),
                  _full_spec(wqn.shape), _full_spec(wqr.shape), _full_spec(wqrr.shape), _full_spec(wuk.shape)],
        out_specs=[row(w) for w, _ in widths] + [pl.BlockSpec((h, tm), lambda i: (0, i)) for h in heights],
        out_shape=[jax.ShapeDtypeStruct((t, w), dt) for w, dt in widths]
                  + [jax.ShapeDtypeStruct((h, t), BF16) for h in heights],
        compiler_params=_cparams(("arbitrary",)),
        name="project",
    )(x, shift, scale, g_mix.reshape(1, -1), w_main, w_rot, cos_t, sin_t,
      g_cq.reshape(1, -1), g_ckv.reshape(1, -1), wqn, wqr, wqrr, wuk)


def _sortable(s):
    b = pltpu.bitcast(s + 0.0, I32)
    return b ^ ((b >> 31) & INT_MAX)


_NT = (((1,), (1,)), ((), ()))


def _fold_lanes(x):
    part = x[:, :LANES]
    for j in range(1, x.shape[1] // LANES):
        part = part + x[:, j * LANES:(j + 1) * LANES]
    return part


def _select_threshold(count, n_sel, rows, idx_bits):
    def bit_step(i, t):
        cand = t ^ lax.shift_left(jnp.int32(1), 31 - i)
        return jnp.where(count(lambda k, c: k >= cand) >= n_sel, cand, t)

    t = lax.fori_loop(0, 32, bit_step, jnp.full((rows, 1), INT_MIN, I32))
    need = n_sel - count(lambda k, c: k > t)
    n_eq = count(lambda k, c: k == t)
    return t, _resolve_ties(count, t, need, n_eq, rows, idx_bits)


def _resolve_ties(count, t, need, n_eq, rows, idx_bits):
    live = t != INT_MIN
    tie = jnp.max(jnp.where((n_eq > need) & live, 1.0, 0.0)) > 0.0

    def tie_search():
        def step(i, j):
            cand = j | lax.shift_left(jnp.int32(1), idx_bits - 1 - i)
            return jnp.where(count(lambda k, c: (k == t) & (c < cand)) <= need, cand, j)
        return lax.fori_loop(0, idx_bits, step, jnp.zeros((rows, 1), I32))

    j = lax.cond(tie, tie_search, lambda: jnp.full((rows, 1), INT_MAX, I32))
    return jnp.where(live, j, 0)


_TRANSPOSE_MASKS = ((16, 0x0000FFFF), (8, 0x00FF00FF), (4, 0x0F0F0F0F), (2, 0x33333333), (1, 0x55555555))
WORD_BITS = 32


def _bit_transpose32(a):
    a = list(a)
    for sh, mask in _TRANSPOSE_MASKS:
        for k in range(WORD_BITS):
            if k & sh:
                continue
            t = (a[k] ^ lax.shift_right_logical(a[k + sh], jnp.int32(sh))) & mask
            a[k] = a[k] ^ t
            a[k + sh] = a[k + sh] ^ lax.shift_left(t, jnp.int32(sh))
    return a


def _popcount_rows(x):
    return jnp.sum(_fold_lanes(lax.population_count(x)).astype(F32), axis=1, keepdims=True)


def _pipelined_chunks(n_chunks, scores, consume):
    scores(0, 0)

    def pair(i, carry):
        c = 2 * i
        scores(c + 1, 1)
        consume(c, 0)
        scores(jnp.minimum(c + 2, n_chunks - 1), 0)
        consume(c + 1, 1)
        return carry

    lax.fori_loop(0, n_chunks // 2, pair, 0)

    @pl.when(n_chunks % 2 == 1)
    def _():
        consume(n_chunks - 1, 0)


def _dsa_prompt_kernel(q_ref, iq_ref, iw_ref, ikt_ref, kt_ref, v_ref, o_ref,
                       key_ref, plane_ref, alive_ref, qbd_ref, s0_ref, s1_ref, m_ref, l_ref, acc_ref,
                       *, tq, ck, cka, n_sel, idx_bits, nblk):
    qb = pl.program_id(0)
    n_att = ((qb + 1) * tq + cka - 1) // cka
    n_chunks = n_att * (cka // ck)
    row_pos = qb * tq + lax.broadcasted_iota(I32, (tq, 1), 0)
    iw = iw_ref[...]
    iq = iq_ref[...]
    iq_h = [iq[:, h * IDX_DIM:(h + 1) * IDX_DIM] for h in range(IDX_HEADS)]
    iw_h = [iw[:, h:h + 1] for h in range(IDX_HEADS)]

    def cols(off):
        return off + lax.broadcasted_iota(I32, (tq, ck), 1)

    def score_chunk(c, carry):
        off = pl.multiple_of(c * ck, ck)
        ikc = ikt_ref[:, pl.ds(off, ck)]
        sc = jnp.zeros((tq, ck), F32)
        for h in range(IDX_HEADS):
            r = jnp.dot(iq_h[h], ikc, preferred_element_type=F32)
            sc = sc + iw_h[h] * jnp.maximum(r, 0.0)
        key_ref[:, pl.ds(off, ck)] = jnp.where(cols(off) <= row_pos, _sortable(sc), INT_MIN)
        return carry

    lax.fori_loop(0, n_chunks, score_chunk, 0)

    def count(pred):
        def body(c, cnt):
            off = pl.multiple_of(c * ck, ck)
            hit = pred(key_ref[:, pl.ds(off, ck)], cols(off))
            return cnt + _fold_lanes(jnp.where(hit, 1.0, 0.0))
        cnt = lax.fori_loop(0, n_chunks, body, jnp.zeros((tq, LANES), F32))
        return jnp.sum(cnt, axis=1, keepdims=True)

    group = nblk * LANES
    n_groups = (n_chunks * ck + group - 1) // group

    def fill(c, carry):
        key_ref[:, pl.ds(pl.multiple_of(c * ck, ck), ck)] = jnp.full((tq, ck), INT_MIN, I32)
        return carry

    lax.fori_loop(n_chunks, n_groups * (group // ck), fill, 0)
    pad = [jnp.full((SUBLANES, LANES), INT_MIN, I32)] * (WORD_BITS - nblk)

    def to_planes(idx, carry):
        g = idx // (tq // SUBLANES)
        r0 = pl.multiple_of((idx % (tq // SUBLANES)) * SUBLANES, SUBLANES)
        c0 = pl.multiple_of(g * group, group)
        words = [key_ref[pl.ds(r0, SUBLANES), pl.ds(c0 + jj * LANES, LANES)] for jj in range(nblk)] + pad
        planes = _bit_transpose32(words)
        planes[0] = ~planes[0]
        for b in range(WORD_BITS):
            plane_ref[b, pl.ds(r0, SUBLANES), pl.ds(pl.multiple_of(g * LANES, LANES), LANES)] = planes[b]
        return carry

    lax.fori_loop(0, n_groups * (tq // SUBLANES), to_planes, 0)
    w_all = alive_ref.shape[1]
    in_span = lax.broadcasted_iota(I32, (tq, w_all), 1) < n_groups * LANES
    alive_ref[...] = jnp.where(in_span, -1, 0).astype(I32)

    def bit_step(b, carry):
        k_rem, t_u = carry
        alive = alive_ref[...]
        ones = alive & plane_ref[b]
        cnt = _popcount_rows(ones)
        take = cnt >= k_rem
        alive_ref[...] = jnp.where(take, ones, alive ^ ones)
        return (jnp.where(take, k_rem, k_rem - cnt),
                jnp.where(take, t_u | lax.shift_left(jnp.int32(1), 31 - b), t_u))

    need, t_u = lax.fori_loop(0, WORD_BITS, bit_step,
                              (jnp.full((tq, 1), float(n_sel), F32), jnp.zeros((tq, 1), I32)))
    t = t_u ^ INT_MIN
    j = _resolve_ties(count, t, need, _popcount_rows(alive_ref[...]), tq, idx_bits)

    q = q_ref[...]
    lane_g = lax.broadcasted_iota(I32, (tq, KV_W), 1) // HEAD_DIM
    for r in range(DSA_GROUP):
        qr = q[:, r * KV_W:(r + 1) * KV_W]
        for g in range(DSA_KV_HEADS):
            hh = r * DSA_KV_HEADS + g
            qbd_ref[hh * tq:(hh + 1) * tq, :] = jnp.where(lane_g == g, qr, jnp.zeros_like(qr))
    _softmax_init(m_ref, l_ref, acc_ref)

    s_bufs = (s0_ref, s1_ref)

    def scores(c, slot):
        off = pl.multiple_of(c * cka, cka)
        s_bufs[slot][...] = jnp.dot(qbd_ref[...], kt_ref[:, pl.ds(off, cka)], preferred_element_type=F32)

    def consume(c, slot):
        off = pl.multiple_of(c * cka, cka)
        kc = key_ref[:, pl.ds(off, cka)]
        col = off + lax.broadcasted_iota(I32, (tq, cka), 1)
        tie_bias = jnp.where(col < j, 0.0, NEG_BIG)
        bias = jnp.where(kc > t, 0.0, jnp.where(kc == t, tie_bias, NEG_BIG))
        s = s_bufs[slot][...] + jnp.concatenate([bias] * DSA_HEADS, axis=0)
        _softmax_step(s, v_ref[pl.ds(off, cka), :], m_ref, l_ref, acc_ref)

    _pipelined_chunks(n_att, scores, consume)
    o_full = acc_ref[...] / l_ref[...]
    for r in range(DSA_GROUP):
        o_r = jnp.zeros((tq, KV_W), F32)
        for g in range(DSA_KV_HEADS):
            hh = r * DSA_KV_HEADS + g
            o_r = o_r + jnp.where(lane_g == g, o_full[hh * tq:(hh + 1) * tq, :], 0.0)
        o_ref[:, r * KV_W:(r + 1) * KV_W] = o_r


def _resident(shape):
    nd = len(shape)
    return pl.BlockSpec(shape, lambda *_: (0,) * nd, pipeline_mode=pl.Buffered(1))


def _dsa_prompt(q16, iq16, iw, ikt16, kt16, v16, tq=128, ck=1024, cka=1024):
    s_len = q16.shape[0]
    cka = min(cka, s_len)
    ck = min(ck, cka)
    n_sel = min(TOPK_MAX, s_len // 4)
    rows = DSA_HEADS * tq
    nblk = min(WORD_BITS, s_len // LANES)
    assert s_len % (nblk * LANES) == 0 and (nblk * LANES) % ck == 0
    n_words = s_len // nblk
    kern = functools.partial(_dsa_prompt_kernel, tq=tq, ck=ck, cka=cka, n_sel=n_sel,
                             idx_bits=int(s_len).bit_length(), nblk=nblk)
    return pl.pallas_call(
        kern,
        grid=(s_len // tq,),
        in_specs=[pl.BlockSpec((tq, DSA_WIDTH), lambda i: (i, 0)),
                  pl.BlockSpec((tq, IDX_HEADS * IDX_DIM), lambda i: (i, 0)),
                  pl.BlockSpec((tq, LANES), lambda i: (i, 0)),
                  _resident(ikt16.shape), _resident(kt16.shape), _resident(v16.shape)],
        out_specs=pl.BlockSpec((tq, DSA_WIDTH), lambda i: (i, 0)),
        out_shape=jax.ShapeDtypeStruct((s_len, DSA_WIDTH), F32),
        scratch_shapes=[pltpu.VMEM((tq, s_len), I32), pltpu.VMEM((WORD_BITS, tq, n_words), I32),
                        pltpu.VMEM((tq, n_words), I32), pltpu.VMEM((rows, KV_W), BF16),
                        pltpu.VMEM((rows, cka), F32), pltpu.VMEM((rows, cka), F32),
                        pltpu.VMEM((rows, 1), F32), pltpu.VMEM((rows, 1), F32), pltpu.VMEM((rows, KV_W), F32)],
        compiler_params=_cparams(("arbitrary",)),
        name="dsa_prompt",
    )(q16, iq16, iw, ikt16, kt16, v16)


def _softmax_step(s, vals, m_ref, l_ref, acc_ref, vals_keys_on_lanes=False):
    m_prev = m_ref[...]
    m_new = jnp.maximum(m_prev, jnp.max(s, axis=1, keepdims=True))
    alpha = jnp.exp(m_prev - m_new)
    p = jnp.exp(s - m_new)
    l_ref[...] = alpha * l_ref[...] + jnp.sum(p, axis=1, keepdims=True)
    if vals_keys_on_lanes:
        pv = lax.dot_general(p.astype(BF16), vals, _NT, preferred_element_type=F32)
    else:
        pv = jnp.dot(p.astype(BF16), vals, preferred_element_type=F32)
    acc_ref[...] = alpha * acc_ref[...] + pv
    m_ref[...] = m_new


def _softmax_init(m_ref, l_ref, acc_ref):
    m_ref[...] = jnp.full(m_ref.shape, NEG_BIG, F32)
    l_ref[...] = jnp.zeros(l_ref.shape, F32)
    acc_ref[...] = jnp.zeros(acc_ref.shape, F32)


def _mla_prompt_kernel(qcat_ref, kcat_ref, kcatt_ref, wuv_ref, o_ref, q_ref, s0_ref, s1_ref, m_ref, l_ref, acc_ref,
                       *, tq, ck):
    qb = pl.program_id(0)
    n_chunks = ((qb + 1) * tq + ck - 1) // ck
    row_pos = qb * tq + lax.broadcasted_iota(I32, (tq, 1), 0)
    for hh in range(MLA_HEADS):
        q_ref[hh * tq:(hh + 1) * tq, :] = qcat_ref[:, hh * QCAT_W:(hh + 1) * QCAT_W]
    _softmax_init(m_ref, l_ref, acc_ref)
    s_bufs = (s0_ref, s1_ref)

    def scores(c, slot):
        off = pl.multiple_of(c * ck, ck)
        s_bufs[slot][...] = jnp.dot(q_ref[...], kcatt_ref[:, pl.ds(off, ck)], preferred_element_type=F32)

    def consume(c, slot):
        off = pl.multiple_of(c * ck, ck)
        col = off + lax.broadcasted_iota(I32, (tq, ck), 1)
        bias = jnp.where(col <= row_pos, 0.0, NEG_BIG)
        s = s_bufs[slot][...] + jnp.concatenate([bias] * MLA_HEADS, axis=0)
        _softmax_step(s, kcat_ref[pl.ds(off, ck), :MLA_KV_LORA], m_ref, l_ref, acc_ref)

    _pipelined_chunks(n_chunks, scores, consume)
    o_lat = (acc_ref[...] / l_ref[...]).astype(BF16)
    lat = jnp.concatenate([o_lat[hh * tq:(hh + 1) * tq, :] for hh in range(MLA_HEADS)], axis=1)
    o_ref[...] = jnp.dot(lat, wuv_ref[...], preferred_element_type=F32)


def _mla_prompt(qcat, kcat, kcatt, wuv, tq=128, ck=1024):
    s_len = qcat.shape[0]
    ck = min(ck, s_len)
    rows = MLA_HEADS * tq
    return pl.pallas_call(
        functools.partial(_mla_prompt_kernel, tq=tq, ck=ck),
        grid=(s_len // tq,),
        in_specs=[pl.BlockSpec((tq, MLA_HEADS * QCAT_W), lambda i: (i, 0)),
                  _resident(kcat.shape), _resident(kcatt.shape), _resident(wuv.shape)],
        out_specs=pl.BlockSpec((tq, MLA_WIDTH), lambda i: (i, 0)),
        out_shape=jax.ShapeDtypeStruct((s_len, MLA_WIDTH), F32),
        scratch_shapes=[pltpu.VMEM((rows, QCAT_W), BF16), pltpu.VMEM((rows, ck), F32), pltpu.VMEM((rows, ck), F32),
                        pltpu.VMEM((rows, 1), F32), pltpu.VMEM((rows, 1), F32), pltpu.VMEM((rows, MLA_KV_LORA), F32)],
        compiler_params=_cparams(("arbitrary",)),
        name="mla_prompt",
    )(qcat, kcat, kcatt, wuv)


PAGES_PER_STEP = 64


def _pad_rows(x):
    return jnp.concatenate([x, jnp.zeros((PAGE_SIZE - x.shape[0], x.shape[1]), x.dtype)], axis=0)


def _tile_rows(x, n):
    return jnp.concatenate([x] * n, axis=0)


def _idx_scores(iq_rows, iw, keys_t16, t):
    r = jnp.dot(iq_rows, keys_t16, preferred_element_type=F32)
    sc = jnp.zeros((t, keys_t16.shape[1]), F32)
    for h in range(IDX_HEADS):
        sc = sc + iw[:, h:h + 1] * jnp.maximum(r[h * t:(h + 1) * t, :], 0.0)
    return sc


def _sample_idx_mla_kernel(pt_ref, iq_ref, iw_ref, ikn_ref, qcat_ref, ckvn_ref, krn_ref, wuv_ref, *rest,
                           pg, n_pages, n_sel, idx_bits, t):
    idx_refs, ckv_refs, kr_refs = rest[:pg], rest[pg:2 * pg], rest[2 * pg:3 * pg]
    o_ref, bias_ref = rest[3 * pg:3 * pg + 2]
    key_ref, iqr_ref, qrow_ref, m_ref, l_ref, acc_ref = rest[3 * pg + 2:]
    j = pl.program_id(1)
    past = n_pages * PAGE_SIZE
    span = pg * PAGE_SIZE

    @pl.when(j == 0)
    def _init():
        iq = iq_ref[...]
        for h in range(IDX_HEADS):
            iqr_ref[h * t:(h + 1) * t, :] = iq[:, h * IDX_DIM:(h + 1) * IDX_DIM]
        qc = qcat_ref[...]
        for hh in range(MLA_HEADS):
            qrow_ref[hh * t:(hh + 1) * t, :] = qc[:, hh * QCAT_W:(hh + 1) * QCAT_W]
        _softmax_init(m_ref, l_ref, acc_ref)

    iqr = iqr_ref[...].astype(BF16)
    iw = iw_ref[...]
    qrow = qrow_ref[...]
    q_lat = qrow[:, :MLA_KV_LORA].astype(BF16)
    q_rope = qrow[:, MLA_KV_LORA + IDX_DIM:MLA_KV_LORA + IDX_DIM + MLA_ROPE].astype(BF16)

    def mla_scores(ck16, kr_t16):
        return (lax.dot_general(q_lat, ck16, _NT, preferred_element_type=F32)
                + jnp.dot(q_rope, kr_t16, preferred_element_type=F32))

    ikc = jnp.concatenate([r[...] for r in idx_refs], axis=1).astype(BF16)
    ckc = jnp.concatenate([r[...] for r in ckv_refs], axis=0).astype(BF16)
    krc = jnp.concatenate([r[...] for r in kr_refs], axis=1).astype(BF16)
    key_ref[:, pl.ds(pl.multiple_of(j * span, span), span)] = _sortable(_idx_scores(iqr, iw, ikc, t))
    _softmax_step(mla_scores(ckc, krc), ckc, m_ref, l_ref, acc_ref)

    @pl.when(j == n_pages // pg - 1)
    def _finish():
        tok = lax.broadcasted_iota(I32, (t, PAGE_SIZE), 0)
        col = lax.broadcasted_iota(I32, (t, PAGE_SIZE), 1)
        vis = col <= tok
        ikn = _pad_rows(ikn_ref[...]).T.astype(BF16)
        key_ref[:, past:past + PAGE_SIZE] = jnp.where(vis, _sortable(_idx_scores(iqr, iw, ikn, t)), INT_MIN)
        ckn = _pad_rows(ckvn_ref[...]).astype(BF16)
        krn = _pad_rows(krn_ref[...]).T.astype(BF16)
        s = mla_scores(ckn, krn) + _tile_rows(jnp.where(vis, 0.0, NEG_BIG), MLA_HEADS)
        _softmax_step(s, ckn, m_ref, l_ref, acc_ref)
        o_lat = acc_ref[...] / l_ref[...]
        lat = jnp.concatenate([o_lat[hh * t:(hh + 1) * t, :] for hh in range(MLA_HEADS)], axis=1)
        o_ref[...] = jnp.dot(lat.astype(BF16), wuv_ref[...], preferred_element_type=F32)

        keys = key_ref[...]
        cols = lax.broadcasted_iota(I32, keys.shape, 1)

        def count(pred):
            return jnp.sum(jnp.where(pred(keys, cols), 1.0, 0.0), axis=1, keepdims=True)

        thr, jj = _select_threshold(count, n_sel, t, idx_bits)
        sel = (keys > thr) | ((keys == thr) & (cols < jj))
        bias_ref[...] = jnp.where(sel, 0.0, NEG_BIG)


def _page_specs(page_shape, pg, n_pages, page_off):
    return [pl.BlockSpec((None,) + page_shape,
                         functools.partial(lambda s, j, pt, i: (page_off + pt[s * n_pages + j * pg + i], 0, 0), i=i))
            for i in range(pg)]


def _key_major(cache, depth_pool):
    return jnp.moveaxis(cache, 2, -1).reshape(depth_pool, -1, PAGE_SIZE)


def _seq_spec(shape):
    return pl.BlockSpec((None,) + tuple(shape[1:]), lambda s, j, pt: (s, 0, 0))


def _sample_idx_mla(pt_flat, iq, iw, ikn, qcat, ckvn, krn, wuv, c_idx, c_ckv, c_kr, n_pages, page_off=0):
    nseq, t, _ = iq.shape
    pg = min(PAGES_PER_STEP, n_pages)
    past = n_pages * PAGE_SIZE
    lp = past + PAGE_SIZE
    n_sel = min(TOPK_MAX, (past + t) // 4)
    rows = MLA_HEADS * t
    kern = functools.partial(_sample_idx_mla_kernel, pg=pg, n_pages=n_pages, n_sel=n_sel,
                             idx_bits=int(past + t).bit_length(), t=t)
    grid_spec = pltpu.PrefetchScalarGridSpec(
        num_scalar_prefetch=1,
        grid=(nseq, n_pages // pg),
        in_specs=[_seq_spec(iq.shape), _seq_spec(iw.shape), _seq_spec(ikn.shape), _seq_spec(qcat.shape),
                  _seq_spec(ckvn.shape), _seq_spec(krn.shape),
                  pl.BlockSpec(wuv.shape, lambda s, j, pt: (0, 0))]
                 + _page_specs((IDX_DIM, PAGE_SIZE), pg, n_pages, page_off)
                 + _page_specs((PAGE_SIZE, MLA_KV_LORA), pg, n_pages, page_off)
                 + _page_specs((MLA_ROPE, PAGE_SIZE), pg, n_pages, page_off),
        out_specs=[pl.BlockSpec((None, t, MLA_WIDTH), lambda s, j, pt: (s, 0, 0)),
                   pl.BlockSpec((None, t, lp), lambda s, j, pt: (s, 0, 0))],
        scratch_shapes=[pltpu.VMEM((t, lp), I32), pltpu.VMEM((IDX_HEADS * t, IDX_DIM), F32),
                        pltpu.VMEM((rows, QCAT_W), F32), pltpu.VMEM((rows, 1), F32),
                        pltpu.VMEM((rows, 1), F32), pltpu.VMEM((rows, MLA_KV_LORA), F32)],
    )
    return pl.pallas_call(
        kern, grid_spec=grid_spec,
        out_shape=[jax.ShapeDtypeStruct((nseq, t, MLA_WIDTH), F32), jax.ShapeDtypeStruct((nseq, t, lp), F32)],
        compiler_params=_cparams(("arbitrary", "arbitrary")),
        name="sample_idx_mla",
    )(pt_flat, iq, iw, ikn, qcat, ckvn, krn, wuv, *([c_idx] * pg), *([c_ckv] * pg), *([c_kr] * pg))


def _sample_dsa_kernel(pt_ref, q_ref, kn_ref, vn_ref, bias_ref, biasn_ref, *rest, pg, n_pages, t):
    k_refs, v_refs = rest[:pg], rest[pg:2 * pg]
    o_ref = rest[2 * pg]
    qbd_ref, m_ref, l_ref, acc_ref = rest[2 * pg + 1:]
    j = pl.program_id(1)
    lane_g = lax.broadcasted_iota(I32, (t, KV_W), 1) // HEAD_DIM

    @pl.when(j == 0)
    def _init():
        q = q_ref[...]
        for r in range(DSA_GROUP):
            qr = q[:, r * KV_W:(r + 1) * KV_W]
            for g in range(DSA_KV_HEADS):
                hh = r * DSA_KV_HEADS + g
                qbd_ref[hh * t:(hh + 1) * t, :] = jnp.where(lane_g == g, qr, 0.0)
        _softmax_init(m_ref, l_ref, acc_ref)

    qbd = qbd_ref[...].astype(BF16)
    kc = jnp.concatenate([r[...] for r in k_refs], axis=1).astype(BF16)
    vc = jnp.concatenate([r[...] for r in v_refs], axis=1).astype(BF16)
    s = jnp.dot(qbd, kc, preferred_element_type=F32) + _tile_rows(bias_ref[...], DSA_HEADS)
    _softmax_step(s, vc, m_ref, l_ref, acc_ref, vals_keys_on_lanes=True)

    @pl.when(j == n_pages // pg - 1)
    def _finish():
        kn = _pad_rows(kn_ref[...]).astype(BF16)
        vn = _pad_rows(vn_ref[...]).astype(BF16)
        s = lax.dot_general(qbd, kn, _NT, preferred_element_type=F32) + _tile_rows(biasn_ref[...], DSA_HEADS)
        _softmax_step(s, vn, m_ref, l_ref, acc_ref)
        o_full = acc_ref[...] / l_ref[...]
        for r in range(DSA_GROUP):
            o_r = jnp.zeros((t, KV_W), F32)
            for g in range(DSA_KV_HEADS):
                hh = r * DSA_KV_HEADS + g
                o_r = o_r + jnp.where(lane_g == g, o_full[hh * t:(hh + 1) * t, :], 0.0)
            o_ref[:, r * KV_W:(r + 1) * KV_W] = o_r


def _sample_dsa(pt_flat, q, kn, vn, bias, c_k, c_v, n_pages, page_off=0):
    nseq, t, _ = q.shape
    pg = min(PAGES_PER_STEP, n_pages)
    rows = DSA_HEADS * t
    span = pg * PAGE_SIZE
    grid_spec = pltpu.PrefetchScalarGridSpec(
        num_scalar_prefetch=1,
        grid=(nseq, n_pages // pg),
        in_specs=[_seq_spec(q.shape), _seq_spec(kn.shape), _seq_spec(vn.shape),
                  pl.BlockSpec((None, t, span), lambda s, j, pt: (s, 0, j)),
                  pl.BlockSpec((None, t, PAGE_SIZE), lambda s, j, pt: (s, 0, n_pages))]
                 + _page_specs((KV_W, PAGE_SIZE), pg, n_pages, page_off)
                 + _page_specs((KV_W, PAGE_SIZE), pg, n_pages, page_off),
        out_specs=pl.BlockSpec((None, t, DSA_WIDTH), lambda s, j, pt: (s, 0, 0)),
        scratch_shapes=[pltpu.VMEM((rows, KV_W), F32), pltpu.VMEM((rows, 1), F32),
                        pltpu.VMEM((rows, 1), F32), pltpu.VMEM((rows, KV_W), F32)],
    )
    return pl.pallas_call(
        functools.partial(_sample_dsa_kernel, pg=pg, n_pages=n_pages, t=t), grid_spec=grid_spec,
        out_shape=jax.ShapeDtypeStruct((nseq, t, DSA_WIDTH), F32),
        compiler_params=_cparams(("arbitrary", "arbitrary")),
        name="sample_dsa",
    )(pt_flat, q, kn, vn, bias, bias, *([c_k] * pg), *([c_v] * pg))


def _merge_kernel(x_ref, od_ref, om_ref, gate_ref, gd_ref, gm_ref, wo_ref, o_ref):
    a = jnp.concatenate([_rms(od_ref[...]) * gd_ref[...], _rms(om_ref[...]) * gm_ref[...]], axis=1)
    y = jnp.dot(a.astype(BF16), wo_ref[...], preferred_element_type=F32)
    o_ref[...] = x_ref[...] + gate_ref[...] * y


def _mod_spec(a, tm):
    if a.shape[0] == 1:
        return pl.BlockSpec((1, a.shape[1]), lambda i, *_: (0, 0))
    return pl.BlockSpec((tm, a.shape[1]), lambda i, *_: (i, 0))


def _merge(x, o_dsa, o_mla, gate, g_dsa, g_mla, w_out16, tm=256):
    t = x.shape[0]
    tm = min(tm, t)

    def row(w):
        return pl.BlockSpec((tm, w), lambda i: (i, 0))

    return pl.pallas_call(
        _merge_kernel,
        grid=(t // tm,),
        in_specs=[row(D_MODEL), row(DSA_WIDTH), row(MLA_WIDTH), _mod_spec(gate, tm),
                  _full_spec((1, DSA_WIDTH)), _full_spec((1, MLA_WIDTH)), _full_spec(w_out16.shape)],
        out_specs=row(D_MODEL),
        out_shape=jax.ShapeDtypeStruct((t, D_MODEL), F32),
        compiler_params=_cparams(("arbitrary",)),
        name="merge",
    )(x, o_dsa, o_mla, gate, g_dsa.reshape(1, -1), g_mla.reshape(1, -1), w_out16)


EXPERT_CHUNK = 2048
I_PER_CHUNK = EXPERT_CHUNK // PEER_NKEYS


def _erf(x):
    return lax.erf(x)


def _extract_top(cur, n):
    vals = []
    for _ in range(n):
        mx = jnp.max(cur, axis=0, keepdims=True)
        vals.append(mx)
        cur = jnp.where(cur == mx, -jnp.inf, cur)
    return vals


def _peer_kernel(x1_ref, shift_ref, scale_ref, gate_ref, gffn_ref, gfin_ref, wpq_ref, keys_ref, u_ref, vt_ref,
                 y_ref, ht_ref, qry_ref, thr_ref, s1_ref, e0_ref, e1_ref, act_ref, ga_ref, acc_ref,
                 *, tm, final_norm):
    c = pl.program_id(1)

    @pl.when(c == 0)
    def _prep():
        h = _rms(x1_ref[...]) * gffn_ref[...] * (1.0 + scale_ref[...]) + shift_ref[...]
        ht_ref[...] = h.T.astype(BF16)
        qry_ref[...] = jnp.dot(wpq_ref[...], ht_ref[...], preferred_element_type=F32)
        acc_ref[...] = jnp.zeros(acc_ref.shape, F32)

        def head(hh, carry):
            subs, tops = [], []
            for p in range(2):
                blk = pl.multiple_of((hh * 2 + p) * PEER_HALF, PEER_HALF)
                sub = jnp.dot(keys_ref[pl.ds(blk, PEER_NKEYS), :], qry_ref[pl.ds(blk, PEER_HALF), :],
                              preferred_element_type=F32, precision=lax.Precision.HIGHEST)
                subs.append(sub)
                tops.append(_extract_top(sub, PEER_TOPK))
            (s0, s1), (top0, top1) = subs, tops
            t1 = jnp.concatenate(top1, axis=0)
            slabs = [top0[a] + t1 for a in range(PEER_TOPK)]
            cand = jnp.concatenate(slabs, axis=0)
            tau = _extract_top(cand, PEER_TOPK)[-1]
            cmax = top0[0] + top1[0]
            z = jnp.sum(jnp.where(cand >= tau, jnp.exp(cand - cmax), 0.0), axis=0, keepdims=True)
            thr = jnp.full((PEER_NKEYS, tm), jnp.inf, F32)
            for a in range(PEER_TOPK):
                thr_a = jnp.min(jnp.where(slabs[a] >= tau, t1, jnp.inf), axis=0, keepdims=True)
                thr = jnp.where(s0 == top0[a], thr_a, thr)
            thr_ref[hh] = thr
            s1_ref[hh] = s1
            e0_ref[hh] = jnp.exp(s0 - top0[0]) / z
            e1_ref[hh] = jnp.exp(s1 - top1[0])
            return carry

        lax.fori_loop(0, PEER_HEADS, head, 0)

    a = jnp.dot(u_ref[...], ht_ref[...], preferred_element_type=F32)
    act_ref[...] = 0.5 * a * (1.0 + _erf(a * (2.0 ** -0.5)))

    for il in range(I_PER_CHUNK):
        i = c * I_PER_CHUNK + il
        g = jnp.zeros((PEER_NKEYS, tm), F32)
        for hh in range(PEER_HEADS):
            hit = s1_ref[hh] >= thr_ref[hh, pl.ds(i, 1), :]
            g = g + jnp.where(hit, e1_ref[hh], 0.0) * e0_ref[hh, pl.ds(i, 1), :]
        rows = slice(il * PEER_NKEYS, (il + 1) * PEER_NKEYS)
        ga_ref[rows, :] = (g * act_ref[rows, :]).astype(BF16)
    acc_ref[...] += jnp.dot(vt_ref[...], ga_ref[...], preferred_element_type=F32)

    @pl.when(c == pl.num_programs(1) - 1)
    def _finish():
        x2 = x1_ref[...] + gate_ref[...] * acc_ref[...].T
        y_ref[...] = _rms(x2) * gfin_ref[...] if final_norm else x2


def _peer_final(x1, shift, scale, gate, g_ffn, g_final, wpq_t16, keys2d, u16, vt16, final_norm, tm=256):
    t = x1.shape[0]
    tm = min(tm, t)
    n_chunks = N_EXPERTS // EXPERT_CHUNK
    hshape = (PEER_HEADS, PEER_NKEYS, tm)
    return pl.pallas_call(
        functools.partial(_peer_kernel, tm=tm, final_norm=final_norm),
        grid=(t // tm, n_chunks),
        in_specs=[pl.BlockSpec((tm, D_MODEL), lambda i, c: (i, 0)),
                  _mod_spec(shift, tm), _mod_spec(scale, tm), _mod_spec(gate, tm),
                  _full_spec((1, D_MODEL)), _full_spec((1, D_MODEL)),
                  _resident(wpq_t16.shape), _resident(keys2d.shape),
                  pl.BlockSpec((EXPERT_CHUNK, D_MODEL), lambda i, c: (c, 0)),
                  pl.BlockSpec((None, D_MODEL, EXPERT_CHUNK), lambda i, c: (c, 0, 0))],
        out_specs=pl.BlockSpec((tm, D_MODEL), lambda i, c: (i, 0)),
        out_shape=jax.ShapeDtypeStruct((t, D_MODEL), F32),
        scratch_shapes=[pltpu.VMEM((D_MODEL, tm), BF16), pltpu.VMEM((2 * PEER_HEADS * PEER_HALF, tm), F32),
                        pltpu.VMEM(hshape, F32), pltpu.VMEM(hshape, F32), pltpu.VMEM(hshape, F32),
                        pltpu.VMEM(hshape, F32),
                        pltpu.VMEM((EXPERT_CHUNK, tm), F32), pltpu.VMEM((EXPERT_CHUNK, tm), BF16),
                        pltpu.VMEM((D_MODEL, tm), F32)],
        compiler_params=_cparams(("arbitrary", "arbitrary")),
        name="peer_final",
    )(x1, shift, scale, gate, g_ffn.reshape(1, -1), g_final.reshape(1, -1), wpq_t16, keys2d, u16, vt16)


def _mod_rows(m, k, n_rep):
    rows = m[:, k * D_MODEL:(k + 1) * D_MODEL]
    return rows if rows.shape[0] == 1 else jnp.repeat(rows, n_rep, axis=0)


def _perm_heads(a):
    tail = a.shape[1:]
    return a.reshape((DSA_KV_HEADS, DSA_GROUP, HEAD_DIM) + tail).swapaxes(0, 1).reshape((DSA_WIDTH,) + tail)


def kernel(x_prompt, x_sample, cache_dsa_k, cache_dsa_v, cache_idx_k, cache_mla_ckv, cache_mla_krope, page_table, c_prompt, c_sample, w_mod, b_mod, g_mix, g_ffn, w_in, g_cq, g_ckv, w_uq, w_uk, w_uv, g_out_dsa, g_out_mla, w_out, w_pq, peer_keys, peer_u, peer_v, g_final):
    depth = w_mod.shape[0]
    b, s_len, _ = x_prompt.shape
    nseq, t, _ = x_sample.shape
    n_pages = page_table.shape[1]
    past = n_pages * PAGE_SIZE
    n_pool = cache_dsa_k.shape[1]
    xp = x_prompt.reshape(b * s_len, D_MODEL)
    xs = x_sample.reshape(nseq * t, D_MODEL)
    cos_p, sin_p = _rope_tables(np.tile(np.arange(s_len), b))
    cos_s, sin_s = _rope_tables(np.tile(past + np.arange(t), nseq))
    pt_flat = page_table.reshape(-1)
    c_k = _key_major(cache_dsa_k, depth * n_pool)
    c_v = _key_major(cache_dsa_v, depth * n_pool)
    c_idx = _key_major(cache_idx_k, depth * n_pool)
    c_kr = _key_major(cache_mla_krope, depth * n_pool)
    c_ckv = cache_mla_ckv.reshape(depth * n_pool, PAGE_SIZE, MLA_KV_LORA)
    new_p = [[] for _ in range(5)]
    new_s = [[] for _ in range(5)]

    for l in range(depth):
        last = l == depth - 1
        mod = _adaln(jnp.concatenate([c_prompt, c_sample], axis=0), w_mod[l], b_mod[l])
        mp, ms = mod[:b], mod[b:]
        wqn, wqr, wqrr, wuk, wuv = _pack_mla_weights(w_uq[l], w_uk[l], w_uv[l])
        packed = _pack_in_weights(w_in[l]) + (wqn, wqr, wqrr, wuk)
        g_dsa = _perm_heads(g_out_dsa[l])
        w_out16 = jnp.concatenate([_perm_heads(w_out[l][:DSA_WIDTH]), w_out[l][DSA_WIDTH:]], axis=0).astype(BF16)
        vt16 = peer_v[l].reshape(N_EXPERTS // EXPERT_CHUNK, EXPERT_CHUNK, D_MODEL).swapaxes(1, 2).astype(BF16)
        peer_w = (w_pq[l].T.astype(BF16), peer_keys[l].reshape(2 * PEER_HEADS * PEER_NKEYS, PEER_HALF),
                  peer_u[l].astype(BF16), vt16)

        (k32, v32, ik32, ckv32, kr32, q16, iq16, iw, v16, kcat, qcat, kt16, ikt16, kcatt) = _project(
            xp, _mod_rows(mp, 0, s_len), _mod_rows(mp, 1, s_len), g_mix[l], packed, cos_p, sin_p, g_cq[l], g_ckv[l])
        o_dsa, o_mla = [], []
        for bi in range(b):
            sl = slice(bi * s_len, (bi + 1) * s_len)
            o_dsa.append(_dsa_prompt(q16[sl], iq16[sl], iw[sl], ikt16[:, sl], kt16[:, sl], v16[sl]))
            o_mla.append(_mla_prompt(qcat[sl], kcat[sl], kcatt[:, sl], wuv))
        o_dsa = o_dsa[0] if b == 1 else jnp.concatenate(o_dsa, axis=0)
        o_mla = o_mla[0] if b == 1 else jnp.concatenate(o_mla, axis=0)
        x1 = _merge(xp, o_dsa, o_mla, _mod_rows(mp, 2, s_len), g_dsa, g_out_mla[l], w_out16)
        xp = _peer_final(x1, _mod_rows(mp, 3, s_len), _mod_rows(mp, 4, s_len), _mod_rows(mp, 5, s_len),
                         g_ffn[l], g_final, *peer_w, final_norm=last)
        for lst, (val, shp) in zip(new_p, ((k32, (DSA_KV_HEADS, HEAD_DIM)), (v32, (DSA_KV_HEADS, HEAD_DIM)),
                                           (ik32, (IDX_DIM,)), (ckv32, (MLA_KV_LORA,)), (kr32, (MLA_ROPE,)))):
            lst.append(val.reshape((b, s_len) + shp))

        ms_tok = [jnp.repeat(ms[:, k * D_MODEL:(k + 1) * D_MODEL], t, axis=0) for k in range(6)]
        (k32, v32, ik32, ckv32, kr32, q16, iq16, iw, v16, kcat, qcat, kt16, ikt16, kcatt) = _project(
            xs, ms_tok[0], ms_tok[1], g_mix[l], packed, cos_s, sin_s, g_cq[l], g_ckv[l], tm=min(256, nseq * t))

        def seq3(a):
            return a.reshape(nseq, t, a.shape[-1]).astype(F32)

        o_mla, bias = _sample_idx_mla(pt_flat, seq3(iq16), seq3(iw), seq3(ik32), seq3(qcat), seq3(ckv32),
                                      seq3(kr32), wuv, c_idx, c_ckv, c_kr, n_pages, page_off=l * n_pool)
        o_dsa = _sample_dsa(pt_flat, seq3(q16), seq3(k32), seq3(v32), bias, c_k, c_v, n_pages,
                            page_off=l * n_pool)
        x1 = _merge(xs, o_dsa.reshape(nseq * t, DSA_WIDTH), o_mla.reshape(nseq * t, MLA_WIDTH), ms_tok[2],
                    g_dsa, g_out_mla[l], w_out16)
        xs = _peer_final(x1, ms_tok[3], ms_tok[4], ms_tok[5], g_ffn[l], g_final, *peer_w, final_norm=last)
        for lst, (val, shp) in zip(new_s, ((k32, (DSA_KV_HEADS, HEAD_DIM)), (v32, (DSA_KV_HEADS, HEAD_DIM)),
                                           (ik32, (IDX_DIM,)), (ckv32, (MLA_KV_LORA,)), (kr32, (MLA_ROPE,)))):
            lst.append(val.reshape((nseq, t) + shp))

    sp = [jnp.stack(a, axis=0) for a in new_p]
    ss = [jnp.stack(a, axis=0) for a in new_s]
    return (xp.reshape(b, s_len, D_MODEL), xs.reshape(nseq, t, D_MODEL),
            sp[0], sp[1], sp[2], sp[3], sp[4], ss[0], ss[1], ss[2], ss[3], ss[4])
```

```python
import functools

import numpy as np
import jax
import jax.numpy as jnp
from jax import lax
from jax.experimental import pallas as pl
from jax.experimental.pallas import tpu as pltpu

F32 = jnp.float32
BF16 = jnp.bfloat16
I32 = jnp.int32

D_MODEL = 1024
HEAD_DIM = 64
DSA_WIDTH = 512
DSA_HEADS = 8
DSA_KV_HEADS = 4
DSA_GROUP = 2
KV_W = DSA_KV_HEADS * HEAD_DIM
IDX_HEADS = 4
IDX_DIM = 64
TOPK_MAX = 256
MLA_V = 64
MLA_WIDTH = 512
MLA_HEADS = 8
MLA_NOPE = 64
MLA_ROPE = 32
MLA_Q_LORA = 384
MLA_KV_LORA = 256
MLA_SCALE = (MLA_NOPE + MLA_ROPE) ** -0.5
PEER_HEADS = 8
PEER_NKEYS = 128
N_EXPERTS = PEER_NKEYS * PEER_NKEYS
PEER_HALF = 128
PEER_TOPK = 16
PAGE_SIZE = 128
ROPE_THETA = 10000.0
EPS = 1e-6
IN_SPLITS = (DSA_WIDTH, KV_W, KV_W, IDX_HEADS * IDX_DIM, IDX_HEADS, IDX_DIM,
             MLA_Q_LORA, MLA_KV_LORA, MLA_ROPE)

LANES = 128
SUBLANES = 8
VMEM_LIMIT = 56 * 1024 * 1024

C_Q, C_K, C_IQ, C_D, C_V, C_CKV, C_CQ, C_IW, C_END = 0, 512, 768, 1024, 1152, 1408, 1664, 2048, 2176
ROT_W = C_V
QCAT_W = 384
INT_MIN = -2147483648
INT_MAX = 2147483647
NEG_BIG = -1e30


def _cparams(sem):
    return pltpu.CompilerParams(dimension_semantics=sem, vmem_limit_bytes=VMEM_LIMIT)


def _full_spec(shape):
    nd = len(shape)
    return pl.BlockSpec(shape, lambda *_: (0,) * nd)


def _adaln_kernel(c_ref, w_ref, b_ref, o_ref):
    c = c_ref[...]
    a = c / (1.0 + jnp.exp(-c))
    o_ref[...] = jnp.dot(a, w_ref[...], preferred_element_type=F32,
                         precision=lax.Precision.HIGHEST) + b_ref[...]


def _adaln(c, w_mod, b_mod):
    n = c.shape[0]
    n_pad = -(-n // SUBLANES) * SUBLANES
    cp = jnp.pad(c, ((0, n_pad - n), (0, 0)))
    tn = 1536
    out = pl.pallas_call(
        _adaln_kernel,
        grid=(w_mod.shape[1] // tn,),
        in_specs=[pl.BlockSpec((n_pad, D_MODEL), lambda j: (0, 0)),
                  pl.BlockSpec((D_MODEL, tn), lambda j: (0, j)),
                  pl.BlockSpec((1, tn), lambda j: (0, j))],
        out_specs=pl.BlockSpec((n_pad, tn), lambda j: (0, j)),
        out_shape=jax.ShapeDtypeStruct((n_pad, w_mod.shape[1]), F32),
        compiler_params=_cparams(("arbitrary",)),
        name="adaln",
    )(cp, w_mod, b_mod.reshape(1, -1))
    return out[:n]


def _swap_halves(w, hd):
    d, n = w.shape
    w4 = w.reshape(d, n // hd, 2, hd // 2)
    return w4[:, :, ::-1, :].reshape(d, n)


def _pack_in_weights(w_in):
    cuts = [int(v) for v in np.cumsum(IN_SPLITS)[:-1]]
    wq, wk, wv, wiq, wiw, wik, wcq, wckv, wkr = jnp.split(w_in, cuts, axis=1)
    wq = wq.reshape(D_MODEL, DSA_KV_HEADS, DSA_GROUP, HEAD_DIM).transpose(0, 2, 1, 3).reshape(D_MODEL, DSA_WIDTH)
    pad32 = jnp.zeros((D_MODEL, 32), F32)
    w_d = jnp.concatenate([wik, wkr, pad32], axis=1)
    w_d_rot = jnp.concatenate([_swap_halves(wik, IDX_DIM), _swap_halves(wkr, MLA_ROPE), pad32], axis=1)
    wiw_p = jnp.pad(wiw, ((0, 0), (0, LANES - IDX_HEADS)))
    w_main = jnp.concatenate([wq, wk, wiq, w_d, wv, wckv, wcq, wiw_p], axis=1)
    w_rot = jnp.concatenate([_swap_halves(wq, HEAD_DIM), _swap_halves(wk, HEAD_DIM),
                             _swap_halves(wiq, IDX_DIM), w_d_rot], axis=1)
    return w_main.astype(BF16), w_rot.astype(BF16)


def _pack_mla_weights(w_uq, w_uk, w_uv):
    w_n = w_uq[:, :, :MLA_NOPE].reshape(MLA_Q_LORA, MLA_HEADS * MLA_NOPE)
    w_r = w_uq[:, :, MLA_NOPE:]
    w_r_rot = jnp.concatenate([w_r[..., MLA_ROPE // 2:], w_r[..., :MLA_ROPE // 2]], axis=-1)

    def place(w):
        z = jnp.zeros((MLA_Q_LORA, MLA_HEADS, LANES), F32)
        return z.at[:, :, IDX_DIM:IDX_DIM + MLA_ROPE].set(w).reshape(MLA_Q_LORA, MLA_HEADS * LANES)

    eye = jnp.eye(MLA_HEADS, dtype=F32)
    uk = jnp.einsum('chd,hk->hdkc', w_uk, eye).reshape(MLA_HEADS * MLA_NOPE, MLA_HEADS * MLA_KV_LORA)
    uv = jnp.einsum('chd,hk->hckd', w_uv, eye).reshape(MLA_HEADS * MLA_KV_LORA, MLA_HEADS * MLA_V)
    return (w_n.astype(BF16), place(w_r).astype(BF16), place(w_r_rot).astype(BF16),
            uk.astype(BF16), uv.astype(BF16))


def _rope_tables(pos):
    pos = np.asarray(pos, np.float32)[:, None]

    def cs(dim):
        half = dim // 2
        inv = np.float32(ROPE_THETA) ** (-np.arange(half, dtype=np.float32) / np.float32(half))
        ang = (pos * inv[None, :]).astype(np.float32)
        c, s = np.cos(ang), np.sin(ang)
        return np.concatenate([c, c], axis=1), np.concatenate([-s, s], axis=1)

    c64, s64 = cs(HEAD_DIM)
    c32, s32 = cs(MLA_ROPE)
    z = np.zeros((pos.shape[0], 32), np.float32)
    return (jnp.asarray(np.concatenate([c64, c64, c64, c32, z], axis=1), F32),
            jnp.asarray(np.concatenate([s64, s64, s64, s32, z], axis=1), F32))


def _rms(x):
    return x * lax.rsqrt(jnp.mean(x * x, axis=-1, keepdims=True) + EPS)


def _proj_kernel(x_ref, shift_ref, scale_ref, gmix_ref, wm_ref, wr_ref, cos_ref, sin_ref,
                 gcq_ref, gckv_ref, wqn_ref, wqr_ref, wqrr_ref, wuk_ref,
                 k32_ref, v32_ref, ik32_ref, ckv32_ref, kr32_ref,
                 q16_ref, iq16_ref, iw_ref, v16_ref, kcat_ref, qcat_ref, kt16_ref, ikt16_ref, kcatt_ref):
    x = x_ref[...]
    h = _rms(x) * gmix_ref[...] * (1.0 + scale_ref[...]) + shift_ref[...]
    hb = h.astype(BF16)
    main = jnp.dot(hb, wm_ref[...], preferred_element_type=F32)
    rot = jnp.dot(hb, wr_ref[...], preferred_element_type=F32)
    cos = cos_ref[...]
    sin = sin_ref[...]
    cos_h, sin_h = cos[:, :LANES], sin[:, :LANES]
    cos_d, sin_d = cos[:, LANES:], sin[:, LANES:]

    def rope(lo, hi):
        reps = (hi - lo) // LANES
        c = jnp.concatenate([cos_h] * reps, axis=1)
        s = jnp.concatenate([sin_h] * reps, axis=1)
        return main[:, lo:hi] * c + rot[:, lo:hi] * s

    q = rope(C_Q, C_K)
    k = rope(C_K, C_IQ)
    iq = rope(C_IQ, C_D)
    sec_d = main[:, C_D:C_V] * cos_d + rot[:, C_D:C_V] * sin_d
    v = main[:, C_V:C_CKV]
    ckv = _rms(main[:, C_CKV:C_CQ]) * gckv_ref[...]
    cqn = (_rms(main[:, C_CQ:C_IW]) * gcq_ref[...]).astype(BF16)

    k32_ref[...] = k
    v32_ref[...] = v
    ik32_ref[...] = sec_d[:, :IDX_DIM]
    ckv32_ref[...] = ckv
    kr32_ref[...] = sec_d[:, IDX_DIM:IDX_DIM + MLA_ROPE]
    q16_ref[...] = (q * HEAD_DIM ** -0.5).astype(BF16)
    iq16_ref[...] = iq.astype(BF16)
    iw_ref[...] = main[:, C_IW:C_END] * (IDX_HEADS ** -0.5 * IDX_DIM ** -0.5)
    v16_ref[...] = v.astype(BF16)
    lane = lax.broadcasted_iota(I32, sec_d.shape, 1)
    kr_blk = jnp.where((lane >= IDX_DIM) & (lane < IDX_DIM + MLA_ROPE), sec_d, 0.0)
    kcat_ref[...] = jnp.concatenate([ckv, kr_blk], axis=1).astype(BF16)
    kt16_ref[...] = k.T.astype(BF16)
    ikt16_ref[...] = sec_d.T[:IDX_DIM, :].astype(BF16)
    kcatt_ref[...] = jnp.concatenate([ckv.T, kr_blk.T], axis=0).astype(BF16)

    qn = jnp.dot(cqn, wqn_ref[...], preferred_element_type=F32).astype(BF16)
    qlat = jnp.dot(qn, wuk_ref[...], preferred_element_type=F32) * MLA_SCALE
    qr_m = jnp.dot(cqn, wqr_ref[...], preferred_element_type=F32)
    qr_r = jnp.dot(cqn, wqrr_ref[...], preferred_element_type=F32)
    cd = jnp.concatenate([cos_d] * MLA_HEADS, axis=1)
    sd = jnp.concatenate([sin_d] * MLA_HEADS, axis=1)
    qr = (qr_m * cd + qr_r * sd) * MLA_SCALE
    for hh in range(MLA_HEADS):
        qcat_ref[:, hh * QCAT_W:hh * QCAT_W + MLA_KV_LORA] = (
            qlat[:, hh * MLA_KV_LORA:(hh + 1) * MLA_KV_LORA].astype(BF16))
        qcat_ref[:, hh * QCAT_W + MLA_KV_LORA:(hh + 1) * QCAT_W] = (
            qr[:, hh * LANES:(hh + 1) * LANES].astype(BF16))


def _project(x, shift, scale, g_mix, packed, cos_t, sin_t, g_cq, g_ckv, tm=256):
    t = x.shape[0]
    w_main, w_rot, wqn, wqr, wqrr, wuk = packed
    per_tok = shift.shape[0] != 1
    mod_spec = pl.BlockSpec((tm, D_MODEL), lambda i: (i, 0)) if per_tok else _full_spec((1, D_MODEL))

    def row(w):
        return pl.BlockSpec((tm, w), lambda i: (i, 0))

    widths = [(KV_W, F32), (KV_W, F32), (IDX_DIM, F32), (MLA_KV_LORA, F32), (MLA_ROPE, F32),
              (DSA_WIDTH, BF16), (IDX_HEADS * IDX_DIM, BF16), (LANES, F32), (KV_W, BF16),
              (QCAT_W, BF16), (MLA_HEADS * QCAT_W, BF16)]
    heights = [KV_W, IDX_DIM, QCAT_W]
    return pl.pallas_call(
        _proj_kernel,
        grid=(t // tm,),
        in_specs=[row(D_MODEL), mod_spec, mod_spec, _full_spec((1, D_MODEL)),
                  _full_spec(w_main.shape), _full_spec(w_rot.shape), row(2 * LANES), row(2 * LANES),
                  _full_spec((1, MLA_Q_LORA)), _full_spec((1, MLA_KV_LORA)),
                  _full_spec(wqn.shape), _full_spec(wqr.shape), _full_spec(wqrr.shape), _full_spec(wuk.shape)],
        out_specs=[row(w) for w, _ in widths] + [pl.BlockSpec((h, tm), lambda i: (0, i)) for h in heights],
        out_shape=[jax.ShapeDtypeStruct((t, w), dt) for w, dt in widths]
                  + [jax.ShapeDtypeStruct((h, t), BF16) for h in heights],
        compiler_params=_cparams(("arbitrary",)),
        name="project",
    )(x, shift, scale, g_mix.reshape(1, -1), w_main, w_rot, cos_t, sin_t,
      g_cq.reshape(1, -1), g_ckv.reshape(1, -1), wqn, wqr, wqrr, wuk)


def _sortable(s):
    b = pltpu.bitcast(s + 0.0, I32)
    return b ^ ((b >> 31) & INT_MAX)


_NT = (((1,), (1,)), ((), ()))


def _fold_lanes(x):
    part = x[:, :LANES]
    for j in range(1, x.shape[1] // LANES):
        part = part + x[:, j * LANES:(j + 1) * LANES]
    return part


def _select_threshold(count, n_sel, rows, idx_bits):
    def bit_step(i, t):
        cand = t ^ lax.shift_left(jnp.int32(1), 31 - i)
        return jnp.where(count(lambda k, c: k >= cand) >= n_sel, cand, t)

    t = lax.fori_loop(0, 32, bit_step, jnp.full((rows, 1), INT_MIN, I32))
    need = n_sel - count(lambda k, c: k > t)
    n_eq = count(lambda k, c: k == t)
    return t, _resolve_ties(count, t, need, n_eq, rows, idx_bits)


def _resolve_ties(count, t, need, n_eq, rows, idx_bits):
    live = t != INT_MIN
    tie = jnp.max(jnp.where((n_eq > need) & live, 1.0, 0.0)) > 0.0

    def tie_search():
        def step(i, j):
            cand = j | lax.shift_left(jnp.int32(1), idx_bits - 1 - i)
            return jnp.where(count(lambda k, c: (k == t) & (c < cand)) <= need, cand, j)
        return lax.fori_loop(0, idx_bits, step, jnp.zeros((rows, 1), I32))

    j = lax.cond(tie, tie_search, lambda: jnp.full((rows, 1), INT_MAX, I32))
    return jnp.where(live, j, 0)


_TRANSPOSE_MASKS = ((16, 0x0000FFFF), (8, 0x00FF00FF), (4, 0x0F0F0F0F), (2, 0x33333333), (1, 0x55555555))
WORD_BITS = 32


def _bit_transpose32(a):
    a = list(a)
    for sh, mask in _TRANSPOSE_MASKS:
        for k in range(WORD_BITS):
            if k & sh:
                continue
            t = (a[k] ^ lax.shift_right_logical(a[k + sh], jnp.int32(sh))) & mask
            a[k] = a[k] ^ t
            a[k + sh] = a[k + sh] ^ lax.shift_left(t, jnp.int32(sh))
    return a


def _popcount_rows(x):
    return jnp.sum(_fold_lanes(lax.population_count(x)).astype(F32), axis=1, keepdims=True)


def _pipelined_chunks(n_chunks, scores, consume):
    scores(0, 0)

    def pair(i, carry):
        c = 2 * i
        scores(c + 1, 1)
        consume(c, 0)
        scores(jnp.minimum(c + 2, n_chunks - 1), 0)
        consume(c + 1, 1)
        return carry

    lax.fori_loop(0, n_chunks // 2, pair, 0)

    @pl.when(n_chunks % 2 == 1)
    def _():
        consume(n_chunks - 1, 0)


def _dsa_prompt_kernel(q_ref, iq_ref, iw_ref, ikt_ref, kt_ref, v_ref, o_ref,
                       key_ref, plane_ref, alive_ref, qbd_ref, s0_ref, s1_ref, m_ref, l_ref, acc_ref,
                       *, tq, ck, cka, n_sel, idx_bits, nblk):
    qb = pl.program_id(0)
    n_att = ((qb + 1) * tq + cka - 1) // cka
    n_chunks = n_att * (cka // ck)
    row_pos = qb * tq + lax.broadcasted_iota(I32, (tq, 1), 0)
    iw = iw_ref[...]
    iq = iq_ref[...]
    iq_h = [iq[:, h * IDX_DIM:(h + 1) * IDX_DIM] for h in range(IDX_HEADS)]
    iw_h = [iw[:, h:h + 1] for h in range(IDX_HEADS)]

    def cols(off):
        return off + lax.broadcasted_iota(I32, (tq, ck), 1)

    def score_chunk(c, carry):
        off = pl.multiple_of(c * ck, ck)
        ikc = ikt_ref[:, pl.ds(off, ck)]
        sc = jnp.zeros((tq, ck), F32)
        for h in range(IDX_HEADS):
            r = jnp.dot(iq_h[h], ikc, preferred_element_type=F32)
            sc = sc + iw_h[h] * jnp.maximum(r, 0.0)
        key_ref[:, pl.ds(off, ck)] = jnp.where(cols(off) <= row_pos, _sortable(sc), INT_MIN)
        return carry

    lax.fori_loop(0, n_chunks, score_chunk, 0)

    def count(pred):
        def body(c, cnt):
            off = pl.multiple_of(c * ck, ck)
            hit = pred(key_ref[:, pl.ds(off, ck)], cols(off))
            return cnt + _fold_lanes(jnp.where(hit, 1.0, 0.0))
        cnt = lax.fori_loop(0, n_chunks, body, jnp.zeros((tq, LANES), F32))
        return jnp.sum(cnt, axis=1, keepdims=True)

    group = nblk * LANES
    n_groups = (n_chunks * ck + group - 1) // group

    def fill(c, carry):
        key_ref[:, pl.ds(pl.multiple_of(c * ck, ck), ck)] = jnp.full((tq, ck), INT_MIN, I32)
        return carry

    lax.fori_loop(n_chunks, n_groups * (group // ck), fill, 0)
    pad = [jnp.full((SUBLANES, LANES), INT_MIN, I32)] * (WORD_BITS - nblk)

    def to_planes(idx, carry):
        g = idx // (tq // SUBLANES)
        r0 = pl.multiple_of((idx % (tq // SUBLANES)) * SUBLANES, SUBLANES)
        c0 = pl.multiple_of(g * group, group)
        words = [key_ref[pl.ds(r0, SUBLANES), pl.ds(c0 + jj * LANES, LANES)] for jj in range(nblk)] + pad
        planes = _bit_transpose32(words)
        planes[0] = ~planes[0]
        for b in range(WORD_BITS):
            plane_ref[b, pl.ds(r0, SUBLANES), pl.ds(pl.multiple_of(g * LANES, LANES), LANES)] = planes[b]
        return carry

    lax.fori_loop(0, n_groups * (tq // SUBLANES), to_planes, 0)
    w_all = alive_ref.shape[1]
    in_span = lax.broadcasted_iota(I32, (tq, w_all), 1) < n_groups * LANES
    alive_ref[...] = jnp.where(in_span, -1, 0).astype(I32)

    def bit_step(b, carry):
        k_rem, t_u = carry
        alive = alive_ref[...]
        ones = alive & plane_ref[b]
        cnt = _popcount_rows(ones)
        take = cnt >= k_rem
        alive_ref[...] = jnp.where(take, ones, alive ^ ones)
        return (jnp.where(take, k_rem, k_rem - cnt),
                jnp.where(take, t_u | lax.shift_left(jnp.int32(1), 31 - b), t_u))

    need, t_u = lax.fori_loop(0, WORD_BITS, bit_step,
                              (jnp.full((tq, 1), float(n_sel), F32), jnp.zeros((tq, 1), I32)))
    t = t_u ^ INT_MIN
    j = _resolve_ties(count, t, need, _popcount_rows(alive_ref[...]), tq, idx_bits)

    q = q_ref[...]
    lane_g = lax.broadcasted_iota(I32, (tq, KV_W), 1) // HEAD_DIM
    for r in range(DSA_GROUP):
        qr = q[:, r * KV_W:(r + 1) * KV_W]
        for g in range(DSA_KV_HEADS):
            hh = r * DSA_KV_HEADS + g
            qbd_ref[hh * tq:(hh + 1) * tq, :] = jnp.where(lane_g == g, qr, jnp.zeros_like(qr))
    _softmax_init(m_ref, l_ref, acc_ref)

    s_bufs = (s0_ref, s1_ref)

    def scores(c, slot):
        off = pl.multiple_of(c * cka, cka)
        s_bufs[slot][...] = jnp.dot(qbd_ref[...], kt_ref[:, pl.ds(off, cka)], preferred_element_type=F32)

    def consume(c, slot):
        off = pl.multiple_of(c * cka, cka)
        kc = key_ref[:, pl.ds(off, cka)]
        col = off + lax.broadcasted_iota(I32, (tq, cka), 1)
        tie_bias = jnp.where(col < j, 0.0, NEG_BIG)
        bias = jnp.where(kc > t, 0.0, jnp.where(kc == t, tie_bias, NEG_BIG))
        s = s_bufs[slot][...] + jnp.concatenate([bias] * DSA_HEADS, axis=0)
        _softmax_step(s, v_ref[pl.ds(off, cka), :], m_ref, l_ref, acc_ref)

    _pipelined_chunks(n_att, scores, consume)
    o_full = acc_ref[...] / l_ref[...]
    for r in range(DSA_GROUP):
        o_r = jnp.zeros((tq, KV_W), F32)
        for g in range(DSA_KV_HEADS):
            hh = r * DSA_KV_HEADS + g
            o_r = o_r + jnp.where(lane_g == g, o_full[hh * tq:(hh + 1) * tq, :], 0.0)
        o_ref[:, r * KV_W:(r + 1) * KV_W] = o_r


def _resident(shape):
    nd = len(shape)
    return pl.BlockSpec(shape, lambda *_: (0,) * nd, pipeline_mode=pl.Buffered(1))


def _dsa_prompt(q16, iq16, iw, ikt16, kt16, v16, tq=128, ck=1024, cka=1024):
    s_len = q16.shape[0]
    cka = min(cka, s_len)
    ck = min(ck, cka)
    n_sel = min(TOPK_MAX, s_len // 4)
    rows = DSA_HEADS * tq
    nblk = min(WORD_BITS, s_len // LANES)
    assert s_len % (nblk * LANES) == 0 and (nblk * LANES) % ck == 0
    n_words = s_len // nblk
    kern = functools.partial(_dsa_prompt_kernel, tq=tq, ck=ck, cka=cka, n_sel=n_sel,
                             idx_bits=int(s_len).bit_length(), nblk=nblk)
    return pl.pallas_call(
        kern,
        grid=(s_len // tq,),
        in_specs=[pl.BlockSpec((tq, DSA_WIDTH), lambda i: (i, 0)),
                  pl.BlockSpec((tq, IDX_HEADS * IDX_DIM), lambda i: (i, 0)),
                  pl.BlockSpec((tq, LANES), lambda i: (i, 0)),
                  _resident(ikt16.shape), _resident(kt16.shape), _resident(v16.shape)],
        out_specs=pl.BlockSpec((tq, DSA_WIDTH), lambda i: (i, 0)),
        out_shape=jax.ShapeDtypeStruct((s_len, DSA_WIDTH), F32),
        scratch_shapes=[pltpu.VMEM((tq, s_len), I32), pltpu.VMEM((WORD_BITS, tq, n_words), I32),
                        pltpu.VMEM((tq, n_words), I32), pltpu.VMEM((rows, KV_W), BF16),
                        pltpu.VMEM((rows, cka), F32), pltpu.VMEM((rows, cka), F32),
                        pltpu.VMEM((rows, 1), F32), pltpu.VMEM((rows, 1), F32), pltpu.VMEM((rows, KV_W), F32)],
        compiler_params=_cparams(("arbitrary",)),
        name="dsa_prompt",
    )(q16, iq16, iw, ikt16, kt16, v16)


def _softmax_step(s, vals, m_ref, l_ref, acc_ref, vals_keys_on_lanes=False):
    m_prev = m_ref[...]
    m_new = jnp.maximum(m_prev, jnp.max(s, axis=1, keepdims=True))
    alpha = jnp.exp(m_prev - m_new)
    p = jnp.exp(s - m_new)
    l_ref[...] = alpha * l_ref[...] + jnp.sum(p, axis=1, keepdims=True)
    if vals_keys_on_lanes:
        pv = lax.dot_general(p.astype(BF16), vals, _NT, preferred_element_type=F32)
    else:
        pv = jnp.dot(p.astype(BF16), vals, preferred_element_type=F32)
    acc_ref[...] = alpha * acc_ref[...] + pv
    m_ref[...] = m_new


def _softmax_init(m_ref, l_ref, acc_ref):
    m_ref[...] = jnp.full(m_ref.shape, NEG_BIG, F32)
    l_ref[...] = jnp.zeros(l_ref.shape, F32)
    acc_ref[...] = jnp.zeros(acc_ref.shape, F32)


def _mla_prompt_kernel(qcat_ref, kcat_ref, kcatt_ref, wuv_ref, o_ref, q_ref, s0_ref, s1_ref, m_ref, l_ref, acc_ref,
                       *, tq, ck):
    qb = pl.program_id(0)
    n_chunks = ((qb + 1) * tq + ck - 1) // ck
    row_pos = qb * tq + lax.broadcasted_iota(I32, (tq, 1), 0)
    for hh in range(MLA_HEADS):
        q_ref[hh * tq:(hh + 1) * tq, :] = qcat_ref[:, hh * QCAT_W:(hh + 1) * QCAT_W]
    _softmax_init(m_ref, l_ref, acc_ref)
    s_bufs = (s0_ref, s1_ref)

    def scores(c, slot):
        off = pl.multiple_of(c * ck, ck)
        s_bufs[slot][...] = jnp.dot(q_ref[...], kcatt_ref[:, pl.ds(off, ck)], preferred_element_type=F32)

    def consume(c, slot):
        off = pl.multiple_of(c * ck, ck)
        col = off + lax.broadcasted_iota(I32, (tq, ck), 1)
        bias = jnp.where(col <= row_pos, 0.0, NEG_BIG)
        s = s_bufs[slot][...] + jnp.concatenate([bias] * MLA_HEADS, axis=0)
        _softmax_step(s, kcat_ref[pl.ds(off, ck), :MLA_KV_LORA], m_ref, l_ref, acc_ref)

    _pipelined_chunks(n_chunks, scores, consume)
    o_lat = (acc_ref[...] / l_ref[...]).astype(BF16)
    lat = jnp.concatenate([o_lat[hh * tq:(hh + 1) * tq, :] for hh in range(MLA_HEADS)], axis=1)
    o_ref[...] = jnp.dot(lat, wuv_ref[...], preferred_element_type=F32)


def _mla_prompt(qcat, kcat, kcatt, wuv, tq=128, ck=1024):
    s_len = qcat.shape[0]
    ck = min(ck, s_len)
    rows = MLA_HEADS * tq
    return pl.pallas_call(
        functools.partial(_mla_prompt_kernel, tq=tq, ck=ck),
        grid=(s_len // tq,),
        in_specs=[pl.BlockSpec((tq, MLA_HEADS * QCAT_W), lambda i: (i, 0)),
                  _resident(kcat.shape), _resident(kcatt.shape), _resident(wuv.shape)],
        out_specs=pl.BlockSpec((tq, MLA_WIDTH), lambda i: (i, 0)),
        out_shape=jax.ShapeDtypeStruct((s_len, MLA_WIDTH), F32),
        scratch_shapes=[pltpu.VMEM((rows, QCAT_W), BF16), pltpu.VMEM((rows, ck), F32), pltpu.VMEM((rows, ck), F32),
                        pltpu.VMEM((rows, 1), F32), pltpu.VMEM((rows, 1), F32), pltpu.VMEM((rows, MLA_KV_LORA), F32)],
        compiler_params=_cparams(("arbitrary",)),
        name="mla_prompt",
    )(qcat, kcat, kcatt, wuv)


PAGES_PER_STEP = 64


def _pad_rows(x):
    return jnp.concatenate([x, jnp.zeros((PAGE_SIZE - x.shape[0], x.shape[1]), x.dtype)], axis=0)


def _tile_rows(x, n):
    return jnp.concatenate([x] * n, axis=0)


def _idx_scores(iq_rows, iw, keys_t16, t):
    r = jnp.dot(iq_rows, keys_t16, preferred_element_type=F32)
    sc = jnp.zeros((t, keys_t16.shape[1]), F32)
    for h in range(IDX_HEADS):
        sc = sc + iw[:, h:h + 1] * jnp.maximum(r[h * t:(h + 1) * t, :], 0.0)
    return sc


def _sample_idx_mla_kernel(pt_ref, iq_ref, iw_ref, ikn_ref, qcat_ref, ckvn_ref, krn_ref, wuv_ref, *rest,
                           pg, n_pages, n_sel, idx_bits, t):
    idx_refs, ckv_refs, kr_refs = rest[:pg], rest[pg:2 * pg], rest[2 * pg:3 * pg]
    o_ref, bias_ref = rest[3 * pg:3 * pg + 2]
    key_ref, iqr_ref, qrow_ref, m_ref, l_ref, acc_ref = rest[3 * pg + 2:]
    j = pl.program_id(1)
    past = n_pages * PAGE_SIZE
    span = pg * PAGE_SIZE

    @pl.when(j == 0)
    def _init():
        iq = iq_ref[...]
        for h in range(IDX_HEADS):
            iqr_ref[h * t:(h + 1) * t, :] = iq[:, h * IDX_DIM:(h + 1) * IDX_DIM]
        qc = qcat_ref[...]
        for hh in range(MLA_HEADS):
            qrow_ref[hh * t:(hh + 1) * t, :] = qc[:, hh * QCAT_W:(hh + 1) * QCAT_W]
        _softmax_init(m_ref, l_ref, acc_ref)

    iqr = iqr_ref[...].astype(BF16)
    iw = iw_ref[...]
    qrow = qrow_ref[...]
    q_lat = qrow[:, :MLA_KV_LORA].astype(BF16)
    q_rope = qrow[:, MLA_KV_LORA + IDX_DIM:MLA_KV_LORA + IDX_DIM + MLA_ROPE].astype(BF16)

    def mla_scores(ck16, kr_t16):
        return (lax.dot_general(q_lat, ck16, _NT, preferred_element_type=F32)
                + jnp.dot(q_rope, kr_t16, preferred_element_type=F32))

    ikc = jnp.concatenate([r[...] for r in idx_refs], axis=1).astype(BF16)
    ckc = jnp.concatenate([r[...] for r in ckv_refs], axis=0).astype(BF16)
    krc = jnp.concatenate([r[...] for r in kr_refs], axis=1).astype(BF16)
    key_ref[:, pl.ds(pl.multiple_of(j * span, span), span)] = _sortable(_idx_scores(iqr, iw, ikc, t))
    _softmax_step(mla_scores(ckc, krc), ckc, m_ref, l_ref, acc_ref)

    @pl.when(j == n_pages // pg - 1)
    def _finish():
        tok = lax.broadcasted_iota(I32, (t, PAGE_SIZE), 0)
        col = lax.broadcasted_iota(I32, (t, PAGE_SIZE), 1)
        vis = col <= tok
        ikn = _pad_rows(ikn_ref[...]).T.astype(BF16)
        key_ref[:, past:past + PAGE_SIZE] = jnp.where(vis, _sortable(_idx_scores(iqr, iw, ikn, t)), INT_MIN)
        ckn = _pad_rows(ckvn_ref[...]).astype(BF16)
        krn = _pad_rows(krn_ref[...]).T.astype(BF16)
        s = mla_scores(ckn, krn) + _tile_rows(jnp.where(vis, 0.0, NEG_BIG), MLA_HEADS)
        _softmax_step(s, ckn, m_ref, l_ref, acc_ref)
        o_lat = acc_ref[...] / l_ref[...]
        lat = jnp.concatenate([o_lat[hh * t:(hh + 1) * t, :] for hh in range(MLA_HEADS)], axis=1)
        o_ref[...] = jnp.dot(lat.astype(BF16), wuv_ref[...], preferred_element_type=F32)

        keys = key_ref[...]
        cols = lax.broadcasted_iota(I32, keys.shape, 1)

        def count(pred):
            return jnp.sum(jnp.where(pred(keys, cols), 1.0, 0.0), axis=1, keepdims=True)

        thr, jj = _select_threshold(count, n_sel, t, idx_bits)
        sel = (keys > thr) | ((keys == thr) & (cols < jj))
        bias_ref[...] = jnp.where(sel, 0.0, NEG_BIG)


def _page_specs(page_shape, pg, n_pages, page_off):
    return [pl.BlockSpec((None,) + page_shape,
                         functools.partial(lambda s, j, pt, i: (page_off + pt[s * n_pages + j * pg + i], 0, 0), i=i))
            for i in range(pg)]


def _key_major(cache, depth_pool):
    return jnp.moveaxis(cache, 2, -1).reshape(depth_pool, -1, PAGE_SIZE)


def _seq_spec(shape):
    return pl.BlockSpec((None,) + tuple(shape[1:]), lambda s, j, pt: (s, 0, 0))


def _sample_idx_mla(pt_flat, iq, iw, ikn, qcat, ckvn, krn, wuv, c_idx, c_ckv, c_kr, n_pages, page_off=0):
    nseq, t, _ = iq.shape
    pg = min(PAGES_PER_STEP, n_pages)
    past = n_pages * PAGE_SIZE
    lp = past + PAGE_SIZE
    n_sel = min(TOPK_MAX, (past + t) // 4)
    rows = MLA_HEADS * t
    kern = functools.partial(_sample_idx_mla_kernel, pg=pg, n_pages=n_pages, n_sel=n_sel,
                             idx_bits=int(past + t).bit_length(), t=t)
    grid_spec = pltpu.PrefetchScalarGridSpec(
        num_scalar_prefetch=1,
        grid=(nseq, n_pages // pg),
        in_specs=[_seq_spec(iq.shape), _seq_spec(iw.shape), _seq_spec(ikn.shape), _seq_spec(qcat.shape),
                  _seq_spec(ckvn.shape), _seq_spec(krn.shape),
                  pl.BlockSpec(wuv.shape, lambda s, j, pt: (0, 0))]
                 + _page_specs((IDX_DIM, PAGE_SIZE), pg, n_pages, page_off)
                 + _page_specs((PAGE_SIZE, MLA_KV_LORA), pg, n_pages, page_off)
                 + _page_specs((MLA_ROPE, PAGE_SIZE), pg, n_pages, page_off),
        out_specs=[pl.BlockSpec((None, t, MLA_WIDTH), lambda s, j, pt: (s, 0, 0)),
                   pl.BlockSpec((None, t, lp), lambda s, j, pt: (s, 0, 0))],
        scratch_shapes=[pltpu.VMEM((t, lp), I32), pltpu.VMEM((IDX_HEADS * t, IDX_DIM), F32),
                        pltpu.VMEM((rows, QCAT_W), F32), pltpu.VMEM((rows, 1), F32),
                        pltpu.VMEM((rows, 1), F32), pltpu.VMEM((rows, MLA_KV_LORA), F32)],
    )
    return pl.pallas_call(
        kern, grid_spec=grid_spec,
        out_shape=[jax.ShapeDtypeStruct((nseq, t, MLA_WIDTH), F32), jax.ShapeDtypeStruct((nseq, t, lp), F32)],
        compiler_params=_cparams(("arbitrary", "arbitrary")),
        name="sample_idx_mla",
    )(pt_flat, iq, iw, ikn, qcat, ckvn, krn, wuv, *([c_idx] * pg), *([c_ckv] * pg), *([c_kr] * pg))


def _sample_dsa_kernel(pt_ref, q_ref, kn_ref, vn_ref, bias_ref, biasn_ref, *rest, pg, n_pages, t):
    k_refs, v_refs = rest[:pg], rest[pg:2 * pg]
    o_ref = rest[2 * pg]
    qbd_ref, m_ref, l_ref, acc_ref = rest[2 * pg + 1:]
    j = pl.program_id(1)
    lane_g = lax.broadcasted_iota(I32, (t, KV_W), 1) // HEAD_DIM

    @pl.when(j == 0)
    def _init():
        q = q_ref[...]
        for r in range(DSA_GROUP):
            qr = q[:, r * KV_W:(r + 1) * KV_W]
            for g in range(DSA_KV_HEADS):
                hh = r * DSA_KV_HEADS + g
                qbd_ref[hh * t:(hh + 1) * t, :] = jnp.where(lane_g == g, qr, 0.0)
        _softmax_init(m_ref, l_ref, acc_ref)

    qbd = qbd_ref[...].astype(BF16)
    kc = jnp.concatenate([r[...] for r in k_refs], axis=1).astype(BF16)
    vc = jnp.concatenate([r[...] for r in v_refs], axis=1).astype(BF16)
    s = jnp.dot(qbd, kc, preferred_element_type=F32) + _tile_rows(bias_ref[...], DSA_HEADS)
    _softmax_step(s, vc, m_ref, l_ref, acc_ref, vals_keys_on_lanes=True)

    @pl.when(j == n_pages // pg - 1)
    def _finish():
        kn = _pad_rows(kn_ref[...]).astype(BF16)
        vn = _pad_rows(vn_ref[...]).astype(BF16)
        s = lax.dot_general(qbd, kn, _NT, preferred_element_type=F32) + _tile_rows(biasn_ref[...], DSA_HEADS)
        _softmax_step(s, vn, m_ref, l_ref, acc_ref)
        o_full = acc_ref[...] / l_ref[...]
        for r in range(DSA_GROUP):
            o_r = jnp.zeros((t, KV_W), F32)
            for g in range(DSA_KV_HEADS):
                hh = r * DSA_KV_HEADS + g
                o_r = o_r + jnp.where(lane_g == g, o_full[hh * t:(hh + 1) * t, :], 0.0)
            o_ref[:, r * KV_W:(r + 1) * KV_W] = o_r


def _sample_dsa(pt_flat, q, kn, vn, bias, c_k, c_v, n_pages, page_off=0):
    nseq, t, _ = q.shape
    pg = min(PAGES_PER_STEP, n_pages)
    rows = DSA_HEADS * t
    span = pg * PAGE_SIZE
    grid_spec = pltpu.PrefetchScalarGridSpec(
        num_scalar_prefetch=1,
        grid=(nseq, n_pages // pg),
        in_specs=[_seq_spec(q.shape), _seq_spec(kn.shape), _seq_spec(vn.shape),
                  pl.BlockSpec((None, t, span), lambda s, j, pt: (s, 0, j)),
                  pl.BlockSpec((None, t, PAGE_SIZE), lambda s, j, pt: (s, 0, n_pages))]
                 + _page_specs((KV_W, PAGE_SIZE), pg, n_pages, page_off)
                 + _page_specs((KV_W, PAGE_SIZE), pg, n_pages, page_off),
        out_specs=pl.BlockSpec((None, t, DSA_WIDTH), lambda s, j, pt: (s, 0, 0)),
        scratch_shapes=[pltpu.VMEM((rows, KV_W), F32), pltpu.VMEM((rows, 1), F32),
                        pltpu.VMEM((rows, 1), F32), pltpu.VMEM((rows, KV_W), F32)],
    )
    return pl.pallas_call(
        functools.partial(_sample_dsa_kernel, pg=pg, n_pages=n_pages, t=t), grid_spec=grid_spec,
        out_shape=jax.ShapeDtypeStruct((nseq, t, DSA_WIDTH), F32),
        compiler_params=_cparams(("arbitrary", "arbitrary")),
        name="sample_dsa",
    )(pt_flat, q, kn, vn, bias, bias, *([c_k] * pg), *([c_v] * pg))


def _merge_kernel(x_ref, od_ref, om_ref, gate_ref, gd_ref, gm_ref, wo_ref, o_ref):
    a = jnp.concatenate([_rms(od_ref[...]) * gd_ref[...], _rms(om_ref[...]) * gm_ref[...]], axis=1)
    y = jnp.dot(a.astype(BF16), wo_ref[...], preferred_element_type=F32)
    o_ref[...] = x_ref[...] + gate_ref[...] * y


def _mod_spec(a, tm):
    if a.shape[0] == 1:
        return pl.BlockSpec((1, a.shape[1]), lambda i, *_: (0, 0))
    return pl.BlockSpec((tm, a.shape[1]), lambda i, *_: (i, 0))


def _merge(x, o_dsa, o_mla, gate, g_dsa, g_mla, w_out16, tm=256):
    t = x.shape[0]
    tm = min(tm, t)

    def row(w):
        return pl.BlockSpec((tm, w), lambda i: (i, 0))

    return pl.pallas_call(
        _merge_kernel,
        grid=(t // tm,),
        in_specs=[row(D_MODEL), row(DSA_WIDTH), row(MLA_WIDTH), _mod_spec(gate, tm),
                  _full_spec((1, DSA_WIDTH)), _full_spec((1, MLA_WIDTH)), _full_spec(w_out16.shape)],
        out_specs=row(D_MODEL),
        out_shape=jax.ShapeDtypeStruct((t, D_MODEL), F32),
        compiler_params=_cparams(("arbitrary",)),
        name="merge",
    )(x, o_dsa, o_mla, gate, g_dsa.reshape(1, -1), g_mla.reshape(1, -1), w_out16)


EXPERT_CHUNK = 2048
I_PER_CHUNK = EXPERT_CHUNK // PEER_NKEYS


def _erf(x):
    return lax.erf(x)


def _extract_top(cur, n):
    vals = []
    for _ in range(n):
        mx = jnp.max(cur, axis=0, keepdims=True)
        vals.append(mx)
        cur = jnp.where(cur == mx, -jnp.inf, cur)
    return vals


def _peer_kernel(x1_ref, shift_ref, scale_ref, gate_ref, gffn_ref, gfin_ref, wpq_ref, keys_ref, u_ref, vt_ref,
                 y_ref, ht_ref, qry_ref, thr_ref, s1_ref, e0_ref, e1_ref, act_ref, ga_ref, acc_ref,
                 *, tm, final_norm):
    c = pl.program_id(1)

    @pl.when(c == 0)
    def _prep():
        h = _rms(x1_ref[...]) * gffn_ref[...] * (1.0 + scale_ref[...]) + shift_ref[...]
        ht_ref[...] = h.T.astype(BF16)
        qry_ref[...] = jnp.dot(wpq_ref[...], ht_ref[...], preferred_element_type=F32)
        acc_ref[...] = jnp.zeros(acc_ref.shape, F32)

        def head(hh, carry):
            subs, tops = [], []
            for p in range(2):
                blk = pl.multiple_of((hh * 2 + p) * PEER_HALF, PEER_HALF)
                sub = jnp.dot(keys_ref[pl.ds(blk, PEER_NKEYS), :], qry_ref[pl.ds(blk, PEER_HALF), :],
                              preferred_element_type=F32, precision=lax.Precision.HIGHEST)
                subs.append(sub)
                tops.append(_extract_top(sub, PEER_TOPK))
            (s0, s1), (top0, top1) = subs, tops
            t1 = jnp.concatenate(top1, axis=0)
            slabs = [top0[a] + t1 for a in range(PEER_TOPK)]
            t0 = jnp.concatenate(top0, axis=0)
            low_a = lax.broadcasted_iota(I32, t0.shape, 0) < 4
            cand = jnp.concatenate(slabs[:4] + [jnp.where(low_a, -jnp.inf, t0 + top1[b]) for b in range(4)], axis=0)
            tau = _extract_top(cand, PEER_TOPK)[-1]
            cmax = top0[0] + top1[0]
            z = jnp.sum(jnp.where(cand >= tau, jnp.exp(cand - cmax), 0.0), axis=0, keepdims=True)
            thr = jnp.full((PEER_NKEYS, tm), jnp.inf, F32)
            for a in range(PEER_TOPK):
                thr_a = jnp.min(jnp.where(slabs[a] >= tau, t1, jnp.inf), axis=0, keepdims=True)
                thr = jnp.where(s0 == top0[a], thr_a, thr)
            thr_ref[hh] = thr
            s1_ref[hh] = s1
            e0_ref[hh] = jnp.exp(s0 - top0[0]) / z
            e1_ref[hh] = jnp.exp(s1 - top1[0])
            return carry

        lax.fori_loop(0, PEER_HEADS, head, 0)

    a = jnp.dot(u_ref[...], ht_ref[...], preferred_element_type=F32)
    act_ref[...] = 0.5 * a * (1.0 + _erf(a * (2.0 ** -0.5)))

    for il in range(I_PER_CHUNK):
        i = c * I_PER_CHUNK + il
        g = jnp.zeros((PEER_NKEYS, tm), F32)
        for hh in range(PEER_HEADS):
            hit = s1_ref[hh] >= thr_ref[hh, pl.ds(i, 1), :]
            g = g + jnp.where(hit, e1_ref[hh], 0.0) * e0_ref[hh, pl.ds(i, 1), :]
        rows = slice(il * PEER_NKEYS, (il + 1) * PEER_NKEYS)
        ga_ref[rows, :] = (g * act_ref[rows, :]).astype(BF16)
    acc_ref[...] += jnp.dot(vt_ref[...], ga_ref[...], preferred_element_type=F32)

    @pl.when(c == pl.num_programs(1) - 1)
    def _finish():
        x2 = x1_ref[...] + gate_ref[...] * acc_ref[...].T
        y_ref[...] = _rms(x2) * gfin_ref[...] if final_norm else x2


def _peer_final(x1, shift, scale, gate, g_ffn, g_final, wpq_t16, keys2d, u16, vt16, final_norm, tm=256):
    t = x1.shape[0]
    tm = min(tm, t)
    n_chunks = N_EXPERTS // EXPERT_CHUNK
    hshape = (PEER_HEADS, PEER_NKEYS, tm)
    return pl.pallas_call(
        functools.partial(_peer_kernel, tm=tm, final_norm=final_norm),
        grid=(t // tm, n_chunks),
        in_specs=[pl.BlockSpec((tm, D_MODEL), lambda i, c: (i, 0)),
                  _mod_spec(shift, tm), _mod_spec(scale, tm), _mod_spec(gate, tm),
                  _full_spec((1, D_MODEL)), _full_spec((1, D_MODEL)),
                  _resident(wpq_t16.shape), _resident(keys2d.shape),
                  pl.BlockSpec((EXPERT_CHUNK, D_MODEL), lambda i, c: (c, 0)),
                  pl.BlockSpec((None, D_MODEL, EXPERT_CHUNK), lambda i, c: (c, 0, 0))],
        out_specs=pl.BlockSpec((tm, D_MODEL), lambda i, c: (i, 0)),
        out_shape=jax.ShapeDtypeStruct((t, D_MODEL), F32),
        scratch_shapes=[pltpu.VMEM((D_MODEL, tm), BF16), pltpu.VMEM((2 * PEER_HEADS * PEER_HALF, tm), F32),
                        pltpu.VMEM(hshape, F32), pltpu.VMEM(hshape, F32), pltpu.VMEM(hshape, F32),
                        pltpu.VMEM(hshape, F32),
                        pltpu.VMEM((EXPERT_CHUNK, tm), F32), pltpu.VMEM((EXPERT_CHUNK, tm), BF16),
                        pltpu.VMEM((D_MODEL, tm), F32)],
        compiler_params=_cparams(("arbitrary", "arbitrary")),
        name="peer_final",
    )(x1, shift, scale, gate, g_ffn.reshape(1, -1), g_final.reshape(1, -1), wpq_t16, keys2d, u16, vt16)


def _mod_rows(m, k, n_rep):
    rows = m[:, k * D_MODEL:(k + 1) * D_MODEL]
    return rows if rows.shape[0] == 1 else jnp.repeat(rows, n_rep, axis=0)


def _perm_heads(a):
    tail = a.shape[1:]
    return a.reshape((DSA_KV_HEADS, DSA_GROUP, HEAD_DIM) + tail).swapaxes(0, 1).reshape((DSA_WIDTH,) + tail)


def kernel(x_prompt, x_sample, cache_dsa_k, cache_dsa_v, cache_idx_k, cache_mla_ckv, cache_mla_krope, page_table, c_prompt, c_sample, w_mod, b_mod, g_mix, g_ffn, w_in, g_cq, g_ckv, w_uq, w_uk, w_uv, g_out_dsa, g_out_mla, w_out, w_pq, peer_keys, peer_u, peer_v, g_final):
    depth = w_mod.shape[0]
    b, s_len, _ = x_prompt.shape
    nseq, t, _ = x_sample.shape
    n_pages = page_table.shape[1]
    past = n_pages * PAGE_SIZE
    n_pool = cache_dsa_k.shape[1]
    xp = x_prompt.reshape(b * s_len, D_MODEL)
    xs = x_sample.reshape(nseq * t, D_MODEL)
    cos_p, sin_p = _rope_tables(np.tile(np.arange(s_len), b))
    cos_s, sin_s = _rope_tables(np.tile(past + np.arange(t), nseq))
    pt_flat = page_table.reshape(-1)
    c_k = _key_major(cache_dsa_k, depth * n_pool)
    c_v = _key_major(cache_dsa_v, depth * n_pool)
    c_idx = _key_major(cache_idx_k, depth * n_pool)
    c_kr = _key_major(cache_mla_krope, depth * n_pool)
    c_ckv = cache_mla_ckv.reshape(depth * n_pool, PAGE_SIZE, MLA_KV_LORA)
    new_p = [[] for _ in range(5)]
    new_s = [[] for _ in range(5)]

    for l in range(depth):
        last = l == depth - 1
        mod = _adaln(jnp.concatenate([c_prompt, c_sample], axis=0), w_mod[l], b_mod[l])
        mp, ms = mod[:b], mod[b:]
        wqn, wqr, wqrr, wuk, wuv = _pack_mla_weights(w_uq[l], w_uk[l], w_uv[l])
        packed = _pack_in_weights(w_in[l]) + (wqn, wqr, wqrr, wuk)
        g_dsa = _perm_heads(g_out_dsa[l])
        w_out16 = jnp.concatenate([_perm_heads(w_out[l][:DSA_WIDTH]), w_out[l][DSA_WIDTH:]], axis=0).astype(BF16)
        vt16 = peer_v[l].reshape(N_EXPERTS // EXPERT_CHUNK, EXPERT_CHUNK, D_MODEL).swapaxes(1, 2).astype(BF16)
        peer_w = (w_pq[l].T.astype(BF16), peer_keys[l].reshape(2 * PEER_HEADS * PEER_NKEYS, PEER_HALF),
                  peer_u[l].astype(BF16), vt16)

        (k32, v32, ik32, ckv32, kr32, q16, iq16, iw, v16, kcat, qcat, kt16, ikt16, kcatt) = _project(
            xp, _mod_rows(mp, 0, s_len), _mod_rows(mp, 1, s_len), g_mix[l], packed, cos_p, sin_p, g_cq[l], g_ckv[l])
        o_dsa, o_mla = [], []
        for bi in range(b):
            sl = slice(bi * s_len, (bi + 1) * s_len)
            o_dsa.append(_dsa_prompt(q16[sl], iq16[sl], iw[sl], ikt16[:, sl], kt16[:, sl], v16[sl]))
            o_mla.append(_mla_prompt(qcat[sl], kcat[sl], kcatt[:, sl], wuv))
        o_dsa = o_dsa[0] if b == 1 else jnp.concatenate(o_dsa, axis=0)
        o_mla = o_mla[0] if b == 1 else jnp.concatenate(o_mla, axis=0)
        x1 = _merge(xp, o_dsa, o_mla, _mod_rows(mp, 2, s_len), g_dsa, g_out_mla[l], w_out16)
        xp = _peer_final(x1, _mod_rows(mp, 3, s_len), _mod_rows(mp, 4, s_len), _mod_rows(mp, 5, s_len),
                         g_ffn[l], g_final, *peer_w, final_norm=last)
        for lst, (val, shp) in zip(new_p, ((k32, (DSA_KV_HEADS, HEAD_DIM)), (v32, (DSA_KV_HEADS, HEAD_DIM)),
                                           (ik32, (IDX_DIM,)), (ckv32, (MLA_KV_LORA,)), (kr32, (MLA_ROPE,)))):
            lst.append(val.reshape((b, s_len) + shp))

        ms_tok = [jnp.repeat(ms[:, k * D_MODEL:(k + 1) * D_MODEL], t, axis=0) for k in range(6)]
        (k32, v32, ik32, ckv32, kr32, q16, iq16, iw, v16, kcat, qcat, kt16, ikt16, kcatt) = _project(
            xs, ms_tok[0], ms_tok[1], g_mix[l], packed, cos_s, sin_s, g_cq[l], g_ckv[l], tm=min(256, nseq * t))

        def seq3(a):
            return a.reshape(nseq, t, a.shape[-1]).astype(F32)

        o_mla, bias = _sample_idx_mla(pt_flat, seq3(iq16), seq3(iw), seq3(ik32), seq3(qcat), seq3(ckv32),
                                      seq3(kr32), wuv, c_idx, c_ckv, c_kr, n_pages, page_off=l * n_pool)
        o_dsa = _sample_dsa(pt_flat, seq3(q16), seq3(k32), seq3(v32), bias, c_k, c_v, n_pages,
                            page_off=l * n_pool)
        x1 = _merge(xs, o_dsa.reshape(nseq * t, DSA_WIDTH), o_mla.reshape(nseq * t, MLA_WIDTH), ms_tok[2],
                    g_dsa, g_out_mla[l], w_out16)
        xs = _peer_final(x1, ms_tok[3], ms_tok[4], ms_tok[5], g_ffn[l], g_final, *peer_w, final_norm=last)
        for lst, (val, shp) in zip(new_s, ((k32, (DSA_KV_HEADS, HEAD_DIM)), (v32, (DSA_KV_HEADS, HEAD_DIM)),
                                           (ik32, (IDX_DIM,)), (ckv32, (MLA_KV_LORA,)), (kr32, (MLA_ROPE,)))):
            lst.append(val.reshape((nseq, t) + shp))

    sp = [jnp.stack(a, axis=0) for a in new_p]
    ss = [jnp.stack(a, axis=0) for a in new_s]
    return (xp.reshape(b, s_len, D_MODEL), xs.reshape(nseq, t, D_MODEL),
            sp[0], sp[1], sp[2], sp[3], sp[4], ss[0], ss[1], ss[2], ss[3], ss[4])
```
